```python
import functools
import jax, jax.numpy as jnp
from jax import lax
import numpy as np

D_MODEL = 1024
BATCH = 8
SEQ = 2048
DEPTH = 4
DEC_BATCH = 32
DEC_SEQ = 8
PAST_LEN = 8192
PAGE_SIZE = 128

N_HEADS = 8
HEAD_DIM = 64
ATT_WIDTH = N_HEADS * HEAD_DIM
POOL_WINDOWS = (2, 4, 8, 16)
N_POOL_GROUPS = len(POOL_WINDOWS)
POOL_WIDTH = 512
POOL_GW = POOL_WIDTH // N_POOL_GROUPS
POOL_BUF = max(POOL_WINDOWS) - 1
D_FF = 2816
Q_BLOCK = 128
ALPHA = (2.0 * DEPTH) ** 0.25
BETA = (8.0 * DEPTH) ** -0.25
LN_EPS = 1e-5
FORGET_BIAS = 4.0
IN_SPLITS = (ATT_WIDTH, 2 * ATT_WIDTH, 3 * ATT_WIDTH, 3 * ATT_WIDTH + N_HEADS,
             3 * ATT_WIDTH + N_HEADS + POOL_WIDTH, 3 * ATT_WIDTH + N_HEADS + POOL_WIDTH + D_MODEL)
IN_COLS = 3 * ATT_WIDTH + N_HEADS + POOL_WIDTH + 2 * D_MODEL

kernel_name = 'fox_pool_macaron_deepnorm_step'


def layer_norm(x, g, b):
    xf = x.astype(jnp.float32)
    mu = xf.mean(-1, keepdims=True)
    var = jnp.square(xf - mu).mean(-1, keepdims=True)
    y = (xf - mu) * lax.rsqrt(var + LN_EPS) * g.astype(jnp.float32) + b.astype(jnp.float32)
    return y.astype(x.dtype)


def swiglu(x, w_up, w_down):
    gate, up = jnp.split(x @ w_up, 2, axis=-1)
    return (jax.nn.silu(gate) * up) @ w_down


def fox_prompt(q, k, v, logf):
    B, T = q.shape[0], q.shape[1]
    nb = T // Q_BLOCK
    scale = HEAD_DIM ** -0.5
    c = jnp.cumsum(logf, axis=1)
    c_keys = c.transpose(0, 2, 1)[:, :, None, :]
    qb = q.reshape(B, nb, Q_BLOCK, N_HEADS, HEAD_DIM).transpose(1, 0, 2, 3, 4)
    cb = c.reshape(B, nb, Q_BLOCK, N_HEADS).transpose(1, 0, 2, 3)
    pos_k = jnp.arange(T)

    def one_block(args):
        i, qi, ci = args
        s = jnp.einsum('bqhd,bkhd->bhqk', qi, k, preferred_element_type=jnp.float32) * scale
        decay = ci.transpose(0, 2, 1)[:, :, :, None].astype(jnp.float32) - c_keys
        pos_q = i * Q_BLOCK + jnp.arange(Q_BLOCK)
        mask = pos_k[None, :] <= pos_q[:, None]
        p = jax.nn.softmax(jnp.where(mask, s + decay, -jnp.inf), axis=-1)
        return jnp.einsum('bhqk,bkhd->bqhd', p.astype(v.dtype), v)

    o = lax.map(one_block, (jnp.arange(nb), qb, cb))
    return o.transpose(1, 0, 2, 3, 4).reshape(B, T, ATT_WIDTH)


def fox_sample(q, k, v, logf, k_past, v_past, logf_past):
    B, S = q.shape[0], q.shape[1]
    scale = HEAD_DIM ** -0.5
    lf_past = logf_past.astype(jnp.float32)
    c_new = jnp.cumsum(logf, axis=1)
    rev = lax.cumsum(lf_past, axis=1, reverse=True) - lf_past
    dec_past = c_new.transpose(0, 2, 1)[:, :, :, None] + rev.transpose(0, 2, 1)[:, :, None, :]
    cn = c_new.transpose(0, 2, 1)
    dec_new = cn[:, :, :, None] - cn[:, :, None, :]
    tri = jnp.tril(jnp.ones((S, S), dtype=bool))
    s_past = jnp.einsum('bqhd,bkhd->bhqk', q, k_past, preferred_element_type=jnp.float32) * scale + dec_past
    s_new = jnp.einsum('bqhd,bkhd->bhqk', q, k, preferred_element_type=jnp.float32) * scale + dec_new
    s_new = jnp.where(tri, s_new, -jnp.inf)
    p = jax.nn.softmax(jnp.concatenate([s_past, s_new], axis=-1), axis=-1)
    P = k_past.shape[1]
    o = (jnp.einsum('bhqk,bkhd->bqhd', p[..., :P].astype(v.dtype), v_past)
         + jnp.einsum('bhqk,bkhd->bqhd', p[..., P:].astype(v.dtype), v))
    return o.reshape(B, S, ATT_WIDTH)


def pool_mix(u_ext, start, w_grp, p_scale):
    B, L, C = u_ext.shape
    T = L - POOL_BUF
    S0 = jnp.concatenate([jnp.zeros((B, 1, C), jnp.float32),
                          jnp.cumsum(u_ext.astype(jnp.float32), axis=1)], axis=1)
    hi = S0[:, POOL_BUF + 1:]
    pos = start + jnp.arange(T)
    means = []
    for gi, w in enumerate(POOL_WINDOWS):
        sl = slice(gi * POOL_GW, (gi + 1) * POOL_GW)
        lo = S0[:, POOL_BUF + 1 - w: POOL_BUF + 1 - w + T, sl]
        cnt = jnp.minimum(pos + 1, w).astype(jnp.float32)[None, :, None]
        means.append((hi[..., sl] - lo) / cnt)
    u = u_ext[:, POOL_BUF:]
    mixed = (jnp.concatenate(means, axis=-1) - u.astype(jnp.float32)).astype(u.dtype)
    mixed = mixed.reshape(B, T, N_POOL_GROUPS, POOL_GW)
    y = jnp.einsum('btgc,gcd->btgd', mixed, w_grp).reshape(B, T, POOL_WIDTH)
    return y * p_scale


def trunk_layer(x, pool_buf, start, attend, prm):
    g, b, w_up, w_down, w_in, b_f, w_grp, p_scale, w_ba, w_bb, w_o = prm
    B, T, _ = x.shape
    x = layer_norm(ALPHA * x + 0.5 * swiglu(x, w_up[0], w_down[0]), g[0], b[0])
    z = x @ w_in
    q, k, v, f_pre, u, g_a, g_b = jnp.split(z, IN_SPLITS, axis=-1)
    q = q.reshape(B, T, N_HEADS, HEAD_DIM)
    k = k.reshape(B, T, N_HEADS, HEAD_DIM)
    v = v.reshape(B, T, N_HEADS, HEAD_DIM)
    logf = jax.nn.log_sigmoid((f_pre + b_f).astype(jnp.float32))
    o_att = attend(q, k, v, logf)
    u_ext = jnp.concatenate([pool_buf.astype(u.dtype), u], axis=1)
    o_pool = pool_mix(u_ext, start, w_grp, p_scale)
    merged = jax.nn.sigmoid(g_a) * (o_att @ w_ba) + jax.nn.sigmoid(g_b) * (o_pool @ w_bb)
    x = layer_norm(ALPHA * x + merged @ w_o, g[1], b[1])
    x = layer_norm(ALPHA * x + 0.5 * swiglu(x, w_up[1], w_down[1]), g[2], b[2])
    return x, k, v, logf, u_ext[:, -POOL_BUF:]


def setup_inputs(seed: int = 0) -> dict:
    key = jax.random.key(seed)
    ks = jax.random.split(key, 20)
    f32 = jnp.float32
    n_pages = PAST_LEN // PAGE_SIZE
    n_pool_pages = (DEC_BATCH * n_pages * 5) // 4

    def nrm(k, shape, s):
        return jax.random.normal(k, shape, f32) * s

    x_prompt = nrm(ks[0], (BATCH, SEQ, D_MODEL), 1.0)
    x_sample = nrm(ks[1], (DEC_BATCH, DEC_SEQ, D_MODEL), 1.0)
    cache_k = nrm(ks[2], (DEPTH, n_pool_pages, PAGE_SIZE, N_HEADS, HEAD_DIM), 1.0)
    cache_v = nrm(ks[3], (DEPTH, n_pool_pages, PAGE_SIZE, N_HEADS, HEAD_DIM), 1.0)
    cache_logf = jax.nn.log_sigmoid(FORGET_BIAS + nrm(ks[4], (DEPTH, n_pool_pages, PAGE_SIZE, N_HEADS), 1.0))
    state_pool = nrm(ks[5], (DEPTH, DEC_BATCH, POOL_BUF, POOL_WIDTH), 1.0)
    perm = jax.random.permutation(ks[6], n_pool_pages)
    page_table = perm[: DEC_BATCH * n_pages].reshape(DEC_BATCH, n_pages).astype(jnp.int32)
    ln_g = 1.0 + nrm(ks[7], (DEPTH, 3, D_MODEL), 0.05)
    ln_b = nrm(ks[8], (DEPTH, 3, D_MODEL), 0.02)
    w_ffn_up = nrm(ks[9], (DEPTH, 2, D_MODEL, 2 * D_FF), D_MODEL ** -0.5)
    w_ffn_down = nrm(ks[10], (DEPTH, 2, D_FF, D_MODEL), BETA * D_FF ** -0.5)
    w_in = nrm(ks[11], (DEPTH, D_MODEL, IN_COLS), D_MODEL ** -0.5)
    b_f = FORGET_BIAS + nrm(ks[12], (DEPTH, N_HEADS), 0.5)
    w_pool_grp = nrm(ks[13], (DEPTH, N_POOL_GROUPS, POOL_GW, POOL_GW), POOL_GW ** -0.5)
    pool_scale = 1.0 + nrm(ks[14], (DEPTH, POOL_WIDTH), 0.1)
    w_branch_a = nrm(ks[15], (DEPTH, ATT_WIDTH, D_MODEL), ATT_WIDTH ** -0.5)
    w_branch_b = nrm(ks[16], (DEPTH, POOL_WIDTH, D_MODEL), POOL_WIDTH ** -0.5)
    w_out = nrm(ks[17], (DEPTH, D_MODEL, D_MODEL), BETA * D_MODEL ** -0.5)
    return {'x_prompt': x_prompt, 'x_sample': x_sample, 'cache_k': cache_k, 'cache_v': cache_v,
            'cache_logf': cache_logf, 'state_pool': state_pool, 'page_table': page_table,
            'ln_g': ln_g, 'ln_b': ln_b, 'w_ffn_up': w_ffn_up, 'w_ffn_down': w_ffn_down,
            'w_in': w_in, 'b_f': b_f, 'w_pool_grp': w_pool_grp, 'pool_scale': pool_scale,
            'w_branch_a': w_branch_a, 'w_branch_b': w_branch_b, 'w_out': w_out}


def reference(x_prompt, x_sample, cache_k, cache_v, cache_logf, state_pool, page_table,
              ln_g, ln_b, w_ffn_up, w_ffn_down, w_in, b_f, w_pool_grp, pool_scale,
              w_branch_a, w_branch_b, w_out):
    dec_b, n_pages = page_table.shape
    past_len = n_pages * cache_k.shape[2]
    xp, xs = x_prompt, x_sample
    pool0 = jnp.zeros((xp.shape[0], POOL_BUF, POOL_WIDTH), xp.dtype)
    kp_l, vp_l, fp_l, bp_l = [], [], [], []
    ks_l, vs_l, fs_l, bs_l = [], [], [], []
    for l in range(DEPTH):
        prm = (ln_g[l], ln_b[l], w_ffn_up[l], w_ffn_down[l], w_in[l], b_f[l],
               w_pool_grp[l], pool_scale[l], w_branch_a[l], w_branch_b[l], w_out[l])
        xp, kp, vp, fp, bp = trunk_layer(xp, pool0, 0, fox_prompt, prm)
        k_past = cache_k[l][page_table].reshape(dec_b, past_len, N_HEADS, HEAD_DIM)
        v_past = cache_v[l][page_table].reshape(dec_b, past_len, N_HEADS, HEAD_DIM)
        f_past = cache_logf[l][page_table].reshape(dec_b, past_len, N_HEADS)
        attend_s = functools.partial(fox_sample, k_past=k_past, v_past=v_past, logf_past=f_past)
        xs, ksm, vsm, fsm, bsm = trunk_layer(xs, state_pool[l], past_len, attend_s, prm)
        kp_l.append(kp); vp_l.append(vp); fp_l.append(fp); bp_l.append(bp)
        ks_l.append(ksm); vs_l.append(vsm); fs_l.append(fsm); bs_l.append(bsm)
    new_k_prompt = jnp.stack(kp_l)
    new_v_prompt = jnp.stack(vp_l)
    new_logf_prompt = jnp.stack(fp_l)
    new_pool_prompt = jnp.stack(bp_l)
    new_k_sample = jnp.stack(ks_l)
    new_v_sample = jnp.stack(vs_l)
    new_logf_sample = jnp.stack(fs_l)
    new_pool_sample = jnp.stack(bs_l)
    return (xp, xs, new_k_prompt, new_v_prompt, new_logf_prompt, new_pool_prompt,
            new_k_sample, new_v_sample, new_logf_sample, new_pool_sample)
```

```python
import functools

import jax
import jax.numpy as jnp
from jax import lax
from jax.experimental import pallas as pl
from jax.experimental.pallas import tpu as pltpu

F32 = jnp.float32
BF16 = jnp.bfloat16

LANES = 128
SUBLANES = 8
VMEM_LIMIT_BYTES = 56 * 1024 * 1024

POOL_WINDOWS = (2, 4, 8, 16)
POOL_HALO = 16
LN_EPS = 1e-5
NEG_BIG = -1e30

ROW_TILE = 512
ATT_TILE = 256
PAGES_PER_STEP = 8


def _cparams(*sem):
    return pltpu.CompilerParams(dimension_semantics=sem, vmem_limit_bytes=VMEM_LIMIT_BYTES)


def _const_spec(shape):
    nd = len(shape)
    return pl.BlockSpec(shape, lambda *_: (0,) * nd)


def _layer_norm(y, g, b):
    mu = jnp.mean(y, axis=-1, keepdims=True)
    d = y - mu
    var = jnp.mean(d * d, axis=-1, keepdims=True)
    return d * lax.rsqrt(var + LN_EPS) * g + b


def _dot(a, b):
    return jnp.dot(a, b, preferred_element_type=F32)


def _dot_nt(a, b):
    return lax.dot_general(a, b, (((1,), (1,)), ((), ())), preferred_element_type=F32)


def _lane_cumsum(x, reverse=False):
    rows, length = x.shape
    lane = lax.broadcasted_iota(jnp.int32, (rows, LANES), 1)
    n = length // LANES
    order = range(n - 1, -1, -1) if reverse else range(n)
    out = [None] * n
    carry = jnp.zeros((rows, 1), F32)
    for c in order:
        v = x[:, c * LANES:(c + 1) * LANES]
        d = 1
        while d < LANES:
            if reverse:
                v = v + jnp.where(lane < LANES - d, pltpu.roll(v, LANES - d, axis=1), 0.0)
            else:
                v = v + jnp.where(lane >= d, pltpu.roll(v, d, axis=1), 0.0)
            d *= 2
        v = v + carry
        carry = v[:, 0:1] if reverse else v[:, LANES - 1:LANES]
        out[c] = v
    return jnp.concatenate(out, axis=1)


def _ffn_ln_kernel(x_ref, wu_ref, wd_ref, g_ref, b_ref, o_ref, *, d_ff, chunk, alpha):
    x = x_ref[...]
    xb = x.astype(BF16)
    acc = jnp.zeros(x.shape, F32)
    for c in range(d_ff // chunk):
        lo = c * chunk
        gate = _dot(xb, wu_ref[:, lo:lo + chunk])
        up = _dot(xb, wu_ref[:, d_ff + lo:d_ff + lo + chunk])
        act = (gate * jax.nn.sigmoid(gate)) * up
        acc = acc + _dot(act.astype(BF16), wd_ref[lo:lo + chunk, :])
    o_ref[...] = _layer_norm(alpha * x + 0.5 * acc, g_ref[...], b_ref[...])


def _ffn_ln(x, w_up, w_down, g, b, *, alpha, tm):
    n, d = x.shape
    d_ff = w_down.shape[0]
    kern = functools.partial(_ffn_ln_kernel, d_ff=d_ff, chunk=256, alpha=alpha)
    return pl.pallas_call(
        kern,
        grid=(n // tm,),
        in_specs=[pl.BlockSpec((tm, d), lambda i: (i, 0)),
                  _const_spec(w_up.shape), _const_spec(w_down.shape),
                  _const_spec((1, d)), _const_spec((1, d))],
        out_specs=pl.BlockSpec((tm, d), lambda i: (i, 0)),
        out_shape=jax.ShapeDtypeStruct((n, d), F32),
        compiler_params=_cparams("parallel"),
        name="ffn_ln",
    )(x, w_up, w_down, g.reshape(1, d), b.reshape(1, d))


def _in_proj_kernel(x_ref, wqkv_ref, wu_ref, wft_ref, bf_ref,
                    qb_ref, k_ref, v_ref, kb_ref, vb_ref, u_ref, lft_ref, *, att, scale):
    xb = x_ref[...].astype(BF16)
    qkv = _dot(xb, wqkv_ref[...])
    q = qkv[:, :att]
    k = qkv[:, att:2 * att]
    v = qkv[:, 2 * att:]
    qb_ref[...] = (q * scale).astype(BF16)
    k_ref[...] = k
    v_ref[...] = v
    kb_ref[...] = k.astype(BF16)
    vb_ref[...] = v.astype(BF16)
    u_ref[...] = _dot(xb, wu_ref[...])
    f = _dot_nt(wft_ref[...], xb) + bf_ref[...]
    lft_ref[...] = jnp.minimum(f, 0.0) - jnp.log1p(jnp.exp(-jnp.abs(f)))


def _in_proj(x, w_qkv, w_u, w_ft, b_f, *, tm, scale):
    n, d = x.shape
    att = w_qkv.shape[1] // 3
    pw = w_u.shape[1]
    nh = w_ft.shape[0]
    row = lambda w: pl.BlockSpec((tm, w), lambda i: (i, 0))
    kern = functools.partial(_in_proj_kernel, att=att, scale=scale)
    return pl.pallas_call(
        kern,
        grid=(n // tm,),
        in_specs=[row(d), _const_spec(w_qkv.shape), _const_spec(w_u.shape),
                  _const_spec(w_ft.shape), _const_spec((nh, 1))],
        out_specs=[row(att), row(att), row(att), row(att), row(att), row(pw),
                   pl.BlockSpec((nh, tm), lambda i: (0, i))],
        out_shape=[jax.ShapeDtypeStruct((n, att), BF16),
                   jax.ShapeDtypeStruct((n, att), F32), jax.ShapeDtypeStruct((n, att), F32),
                   jax.ShapeDtypeStruct((n, att), BF16), jax.ShapeDtypeStruct((n, att), BF16),
                   jax.ShapeDtypeStruct((n, pw), F32),
                   jax.ShapeDtypeStruct((nh, n), F32)],
        compiler_params=_cparams("parallel"),
        name="in_proj",
    )(x, w_qkv, w_u, w_ft, b_f.reshape(nh, 1))


def _attn_prompt_kernel(lf_ref, q_ref, k_ref, v_ref, o_ref, c_scr, *, tile, head_dim):
    qi = pl.program_id(2)

    @pl.when(qi == 0)
    def _():
        c_scr[...] = _lane_cumsum(lf_ref[...])

    q = q_ref[...]
    lane = lax.broadcasted_iota(jnp.int32, q.shape, 1)
    first = lane < head_dim
    zero = jnp.zeros_like(q)
    q_heads = (jnp.where(first, q, zero), jnp.where(first, zero, q))
    q0 = pl.multiple_of(qi * tile, tile)
    base = c_scr[:, pl.ds(q0, LANES)][:, 0:1]

    def block(j, carry, masked):
        k0 = pl.multiple_of(j * tile, tile)
        kb = k_ref[pl.ds(k0, tile), :]
        vb = v_ref[pl.ds(k0, tile), :]
        bias = base - c_scr[:, pl.ds(k0, tile)]
        out = []
        for h in range(2):
            m, l, acc = carry[h]
            s = _dot_nt(q_heads[h], kb) + bias[h:h + 1, :]
            if masked:
                r = lax.broadcasted_iota(jnp.int32, s.shape, 0)
                c = lax.broadcasted_iota(jnp.int32, s.shape, 1)
                s = jnp.where(c <= r, s, NEG_BIG)
            m_new = jnp.maximum(m, jnp.max(s, axis=-1, keepdims=True))
            p = jnp.exp(s - m_new)
            corr = jnp.exp(m - m_new)
            l = l * corr + jnp.sum(p, axis=-1, keepdims=True)
            acc = acc * corr + _dot(p.astype(BF16), vb)
            out.append((m_new, l, acc))
        return tuple(out)

    init = tuple((jnp.full((tile, 1), NEG_BIG, F32), jnp.zeros((tile, 1), F32),
                  jnp.zeros((tile, 2 * head_dim), F32)) for _ in range(2))
    carry = lax.fori_loop(0, qi, lambda j, c: block(j, c, False), init)
    (_, l0, a0), (_, l1, a1) = block(qi, carry, True)
    o_ref[...] = jnp.where(first, a0 * (1.0 / l0), a1 * (1.0 / l1)).astype(o_ref.dtype)


def _attn_prompt(qb, kb, vb, lf4, *, batch, seq, head_dim):
    n, att = qb.shape
    pairs = att // (2 * head_dim)
    tile = ATT_TILE
    nq = seq // tile
    w = 2 * head_dim
    kern = functools.partial(_attn_prompt_kernel, tile=tile, head_dim=head_dim)
    return pl.pallas_call(
        kern,
        grid=(batch, pairs, nq),
        in_specs=[pl.BlockSpec((None, 2, seq), lambda b, h, i: (h, 0, b)),
                  pl.BlockSpec((tile, w), lambda b, h, i: (b * nq + i, h)),
                  pl.BlockSpec((seq, w), lambda b, h, i: (b, h)),
                  pl.BlockSpec((seq, w), lambda b, h, i: (b, h))],
        out_specs=pl.BlockSpec((tile, w), lambda b, h, i: (b * nq + i, h)),
        out_shape=jax.ShapeDtypeStruct((n, att), BF16),
        scratch_shapes=[pltpu.VMEM((2, seq), F32)],
        compiler_params=_cparams("parallel", "parallel", "arbitrary"),
        name="attn_prompt",
    )(lf4, qb, kb, vb)


def _attn_sample_kernel(pt_ref, q_ref, lfp_ref, lfn_ref, kn_ref, vn_ref, *rest,
                        pages, heads, head_dim, page_size):
    del pt_ref
    k_refs = rest[:pages]
    v_refs = rest[pages:2 * pages]
    o_ref = rest[2 * pages]
    rev_scr, m_scr, l_scr, acc_scr = rest[2 * pages + 1:]
    g = pl.program_id(1)
    nq = q_ref.shape[0]

    @pl.when(g == 0)
    def _():
        lfp = lfp_ref[...]
        rev_scr[...] = _lane_cumsum(lfp, reverse=True) - lfp
        m_scr[...] = jnp.full(m_scr.shape, NEG_BIG, F32)
        l_scr[...] = jnp.zeros(l_scr.shape, F32)
        acc_scr[...] = jnp.zeros(acc_scr.shape, F32)

    q = q_ref[...]
    q_heads = [q[:, h * head_dim:(h + 1) * head_dim] for h in range(heads)]

    def head_rows(ref, h):
        return ref[pl.ds(h, page_size, stride=heads), :].astype(BF16)

    def expand(x):
        return jnp.concatenate(
            [jnp.broadcast_to(x[h:h + 1, :], (nq, x.shape[1])) for h in range(heads)], axis=0)

    def process(krefs, vrefs, bias, mask):
        s = jnp.concatenate(
            [jnp.concatenate([_dot_nt(q_heads[h], head_rows(kr, h)) for kr in krefs], axis=1)
             for h in range(heads)], axis=0) + bias
        if mask is not None:
            s = jnp.where(mask, s, NEG_BIG)
        m_old = m_scr[...]
        m_new = jnp.maximum(m_old, jnp.max(s, axis=-1, keepdims=True))
        p = jnp.exp(s - m_new)
        corr = jnp.exp(m_old - m_new)
        l_scr[...] = l_scr[...] * corr + jnp.sum(p, axis=-1, keepdims=True)
        m_scr[...] = m_new
        pb = p.astype(BF16)
        pv = []
        for h in range(heads):
            tot = None
            for j, vr in enumerate(vrefs):
                t = _dot(pb[h * nq:(h + 1) * nq, j * page_size:(j + 1) * page_size],
                         head_rows(vr, h))
                tot = t if tot is None else tot + t
            pv.append(tot)
        acc_scr[...] = acc_scr[...] * corr + jnp.concatenate(pv, axis=0)

    span = pages * page_size
    g0 = pl.multiple_of(g * span, span)
    process(k_refs, v_refs, expand(rev_scr[:, pl.ds(g0, span)]), None)

    @pl.when(g == pl.num_programs(1) - 1)
    def _():
        c_new = _lane_cumsum(lfn_ref[...])
        rows = lax.broadcasted_iota(jnp.int32, (heads * nq, page_size), 0)
        cols = lax.broadcasted_iota(jnp.int32, (heads * nq, page_size), 1)
        process([kn_ref], [vn_ref], -expand(c_new), cols <= rows % nq)
        out = acc_scr[...] * (1.0 / l_scr[...])
        o_ref[...] = jnp.concatenate(
            [out[h * nq:(h + 1) * nq, :] for h in range(heads)], axis=1).astype(o_ref.dtype)


def _attn_sample(layer, page_table, qs, lfp_t, lfn_t, k_new, v_new, cache_k2, cache_v2,
                 *, heads, head_dim, page_size):
    nb, nq, att = qs.shape
    n_pages = page_table.shape[1]
    pages = PAGES_PER_STEP
    past = n_pages * page_size
    page_rows = page_size * heads

    def page_spec(j):
        return pl.BlockSpec((None, None, page_rows, head_dim),
                            lambda b, g, pt: (layer, pt[b, g * pages + j], 0, 0))

    per_seq = lambda shape: pl.BlockSpec((None,) + shape, lambda b, g, pt: (b, 0, 0))
    kern = functools.partial(_attn_sample_kernel, pages=pages, heads=heads,
                             head_dim=head_dim, page_size=page_size)
    grid_spec = pltpu.PrefetchScalarGridSpec(
        num_scalar_prefetch=1,
        grid=(nb, n_pages // pages),
        in_specs=[per_seq((nq, att)), per_seq((heads, past)), per_seq((heads, LANES)),
                  per_seq((page_rows, head_dim)), per_seq((page_rows, head_dim))]
                 + [page_spec(j) for j in range(pages)] * 2,
        out_specs=per_seq((nq, att)),
        scratch_shapes=[pltpu.VMEM((heads, past), F32),
                        pltpu.VMEM((heads * nq, 1), F32), pltpu.VMEM((heads * nq, 1), F32),
                        pltpu.VMEM((heads * nq, head_dim), F32)],
    )
    return pl.pallas_call(
        kern,
        grid_spec=grid_spec,
        out_shape=jax.ShapeDtypeStruct((nb, nq, att), BF16),
        compiler_params=_cparams("parallel", "arbitrary"),
        name="attn_sample",
    )(page_table, qs, lfp_t, lfn_t, k_new, v_new,
      *([cache_k2] * pages), *([cache_v2] * pages))


def _window_means_minus_self(ext, u, pos, axis):
    halo = ext.shape[axis] - u.shape[axis]
    gw = u.shape[-1] // len(POOL_WINDOWS)
    outs = []
    for gi, w in enumerate(POOL_WINDOWS):
        a = ext[..., gi * gw:(gi + 1) * gw]
        d = 1
        while d < w:
            a = a + pltpu.roll(a, d, axis=axis)
            d *= 2
        a = lax.slice_in_dim(a, halo, ext.shape[axis], axis=axis)
        inv = 1.0 / jnp.minimum(pos + 1, w).astype(F32)
        outs.append(a * inv - u[..., gi * gw:(gi + 1) * gw])
    return outs


def _merge_tail(x, o_att, mixed, wg_ref, wgrp_ref, ps_ref, wba_ref, wbb_ref, wo_ref,
                g_ref, b_ref, *, alpha):
    xb = x.astype(BF16)
    d = x.shape[-1]
    gates = jax.nn.sigmoid(_dot(xb, wg_ref[...]))
    pooled = jnp.concatenate(
        [_dot(mx.astype(BF16), wgrp_ref[gi]) for gi, mx in enumerate(mixed)], axis=-1)
    pooled = pooled * ps_ref[...]
    merged = (gates[:, :d] * _dot(o_att, wba_ref[...])
              + gates[:, d:] * _dot(pooled.astype(BF16), wbb_ref[...]))
    y = alpha * x + _dot(merged.astype(BF16), wo_ref[...])
    return _layer_norm(y, g_ref[...], b_ref[...])


def _merge_prompt_kernel(x_ref, oa_ref, u_ref, uh_ref, *rest, alpha, tiles_per_seq):
    o_ref = rest[-1]
    i = pl.program_id(0)
    t = i % tiles_per_seq
    u = u_ref[...]
    tm = u.shape[0]
    halo = jnp.where(t == 0, 0.0, uh_ref[...])
    ext = jnp.concatenate([halo, u], axis=0)
    pos = t * tm + lax.broadcasted_iota(jnp.int32, (tm, 1), 0)
    mixed = _window_means_minus_self(ext, u, pos, axis=0)
    o_ref[...] = _merge_tail(x_ref[...], oa_ref[...], mixed, *rest[:-1], alpha=alpha)


def _merge_sample_kernel(x_ref, oa_ref, u_ref, st_ref, *rest, alpha, start):
    o_ref = rest[-1]
    u = u_ref[...]
    nb, ns, ch = u.shape
    ext = jnp.concatenate([st_ref[...], u], axis=1)
    pos = start + lax.broadcasted_iota(jnp.int32, (1, ns, 1), 1)
    mixed = [m.reshape(nb * ns, -1) for m in _window_means_minus_self(ext, u, pos, axis=1)]
    o_ref[...] = _merge_tail(x_ref[...], oa_ref[...], mixed, *rest[:-1], alpha=alpha)


def _merge_weight_specs(ws):
    return [_const_spec(w.shape) for w in ws]


def _merge_prompt(x, o_att, u, weights, *, alpha, tm, seq):
    n, d = x.shape
    att = o_att.shape[1]
    pw = u.shape[1]
    hb = tm // POOL_HALO
    kern = functools.partial(_merge_prompt_kernel, alpha=alpha, tiles_per_seq=seq // tm)
    return pl.pallas_call(
        kern,
        grid=(n // tm,),
        in_specs=[pl.BlockSpec((tm, d), lambda i: (i, 0)),
                  pl.BlockSpec((tm, att), lambda i: (i, 0)),
                  pl.BlockSpec((tm, pw), lambda i: (i, 0)),
                  pl.BlockSpec((POOL_HALO, pw), lambda i: (jnp.maximum(i * hb - 1, 0), 0))]
                 + _merge_weight_specs(weights),
        out_specs=pl.BlockSpec((tm, d), lambda i: (i, 0)),
        out_shape=jax.ShapeDtypeStruct((n, d), F32),
        compiler_params=_cparams("parallel"),
        name="merge_prompt",
    )(x, o_att, u, u, *weights)


def _merge_sample(x, o_att, u3, state, weights, *, alpha, start):
    n, d = x.shape
    kern = functools.partial(_merge_sample_kernel, alpha=alpha, start=start)
    return pl.pallas_call(
        kern,
        grid=(1,),
        in_specs=[_const_spec(x.shape), _const_spec(o_att.shape), _const_spec(u3.shape),
                  _const_spec(state.shape)] + _merge_weight_specs(weights),
        out_specs=_const_spec((n, d)),
        out_shape=jax.ShapeDtypeStruct((n, d), F32),
        compiler_params=_cparams("arbitrary"),
        name="merge_sample",
    )(x, o_att, u3, state, *weights)


def kernel(x_prompt, x_sample, cache_k, cache_v, cache_logf, state_pool, page_table,
           ln_g, ln_b, w_ffn_up, w_ffn_down, w_in, b_f, w_pool_grp, pool_scale,
           w_branch_a, w_branch_b, w_out):
    batch, seq, d = x_prompt.shape
    dec_b, dec_s, _ = x_sample.shape
    depth, pool_pages, page_size, heads, head_dim = cache_k.shape
    att = heads * head_dim
    pool_w = pool_scale.shape[1]
    pool_buf = state_pool.shape[2]
    n_pages = page_table.shape[1]
    past = n_pages * page_size
    alpha = (2.0 * depth) ** 0.25
    scale = head_dim ** -0.5
    np_rows = batch * seq
    ns_rows = dec_b * dec_s
    assert pool_buf == max(POOL_WINDOWS) - 1 and pool_w % (len(POOL_WINDOWS) * LANES) == 0
    assert seq % ROW_TILE == 0 and seq % ATT_TILE == 0 and n_pages % PAGES_PER_STEP == 0
    assert dec_s == SUBLANES and heads % 2 == 0 and 2 * head_dim == LANES

    cache_k2 = cache_k.reshape(depth, pool_pages, page_size * heads, head_dim)
    cache_v2 = cache_v.reshape(depth, pool_pages, page_size * heads, head_dim)

    xp = x_prompt.reshape(np_rows, d)
    xs = x_sample.reshape(ns_rows, d)
    outs = {k: [] for k in ("kp", "vp", "fp", "bp", "ks", "vs", "fs", "bs")}

    for l in range(depth):
        wu = w_ffn_up[l].astype(BF16)
        wd = w_ffn_down[l].astype(BF16)
        wi = w_in[l]
        w_qkv = wi[:, :3 * att].astype(BF16)
        w_ft = wi[:, 3 * att:3 * att + heads].T.astype(BF16)
        c0 = 3 * att + heads
        w_u = wi[:, c0:c0 + pool_w].astype(BF16)
        w_g = wi[:, c0 + pool_w:].astype(BF16)
        merge_w = (w_g, w_pool_grp[l].astype(BF16), pool_scale[l].reshape(1, pool_w),
                   w_branch_a[l].astype(BF16), w_branch_b[l].astype(BF16),
                   w_out[l].astype(BF16), ln_g[l, 1].reshape(1, d), ln_b[l, 1].reshape(1, d))

        x1 = _ffn_ln(xp, wu[0], wd[0], ln_g[l, 0], ln_b[l, 0], alpha=alpha, tm=ROW_TILE)
        qb, k, v, kb, vb, u, lft = _in_proj(x1, w_qkv, w_u, w_ft, b_f[l], tm=ROW_TILE, scale=scale)
        o_att = _attn_prompt(qb, kb, vb, lft.reshape(heads // 2, 2, np_rows),
                             batch=batch, seq=seq, head_dim=head_dim)
        x2 = _merge_prompt(x1, o_att, u, merge_w, alpha=alpha, tm=ROW_TILE, seq=seq)
        xp = _ffn_ln(x2, wu[1], wd[1], ln_g[l, 2], ln_b[l, 2], alpha=alpha, tm=ROW_TILE)
        outs["kp"].append(k.reshape(batch, seq, heads, head_dim))
        outs["vp"].append(v.reshape(batch, seq, heads, head_dim))
        outs["fp"].append(lft.reshape(heads, batch, seq).transpose(1, 2, 0))
        outs["bp"].append(u.reshape(batch, seq, pool_w)[:, seq - pool_buf:])

        y1 = _ffn_ln(xs, wu[0], wd[0], ln_g[l, 0], ln_b[l, 0], alpha=alpha, tm=ns_rows)
        qb, k, v, _, _, u, lft = _in_proj(y1, w_qkv, w_u, w_ft, b_f[l], tm=ns_rows, scale=scale)
        lf_new = lft.reshape(heads, dec_b, dec_s).transpose(1, 0, 2)
        lfn_t = jnp.pad(lf_new, ((0, 0), (0, 0), (0, LANES - dec_s)))
        lfp_t = cache_logf[l][page_table].reshape(dec_b, past, heads).transpose(0, 2, 1)

        def page_layout(a):
            a = a.reshape(dec_b, dec_s * heads, head_dim)
            return jnp.pad(a, ((0, 0), (0, (page_size - dec_s) * heads), (0, 0)))

        o_att = _attn_sample(l, page_table, qb.reshape(dec_b, dec_s, att), lfp_t, lfn_t,
                             page_layout(k), page_layout(v), cache_k2, cache_v2,
                             heads=heads, head_dim=head_dim, page_size=page_size)
        state = jnp.pad(state_pool[l], ((0, 0), (POOL_HALO - pool_buf, 0), (0, 0)))
        u3 = u.reshape(dec_b, dec_s, pool_w)
        y2 = _merge_sample(y1, o_att.reshape(ns_rows, att), u3, state, merge_w,
                           alpha=alpha, start=past)
        xs = _ffn_ln(y2, wu[1], wd[1], ln_g[l, 2], ln_b[l, 2], alpha=alpha, tm=ns_rows)
        outs["ks"].append(k.reshape(dec_b, dec_s, heads, head_dim))
        outs["vs"].append(v.reshape(dec_b, dec_s, heads, head_dim))
        outs["fs"].append(lf_new.transpose(0, 2, 1))
        outs["bs"].append(jnp.concatenate([state_pool[l], u3], axis=1)[:, dec_s:])

    st = {k: jnp.stack(v) for k, v in outs.items()}
    return (xp.reshape(batch, seq, d), xs.reshape(dec_b, dec_s, d),
            st["kp"], st["vp"], st["fp"], st["bp"], st["ks"], st["vs"], st["fs"], st["bs"])
```

```python
import functools

import jax
import jax.numpy as jnp
from jax import lax
from jax.experimental import pallas as pl
from jax.experimental.pallas import tpu as pltpu

F32 = jnp.float32
BF16 = jnp.bfloat16

LANES = 128
SUBLANES = 8
VMEM_LIMIT_BYTES = 56 * 1024 * 1024

POOL_WINDOWS = (2, 4, 8, 16)
POOL_HALO = 16
LN_EPS = 1e-5
NEG_BIG = -1e30

ROW_TILE = 512
ATT_TILE = 512
PAGES_PER_STEP = 8


def _cparams(*sem):
    return pltpu.CompilerParams(dimension_semantics=sem, vmem_limit_bytes=VMEM_LIMIT_BYTES)


def _const_spec(shape):
    nd = len(shape)
    return pl.BlockSpec(shape, lambda *_: (0,) * nd)


def _layer_norm(y, g, b):
    mu = jnp.mean(y, axis=-1, keepdims=True)
    d = y - mu
    var = jnp.mean(d * d, axis=-1, keepdims=True)
    return d * lax.rsqrt(var + LN_EPS) * g + b


def _dot(a, b):
    return jnp.dot(a, b, preferred_element_type=F32)


def _dot_nt(a, b):
    return lax.dot_general(a, b, (((1,), (1,)), ((), ())), preferred_element_type=F32)


def _lane_cumsum(x, reverse=False):
    rows, length = x.shape
    lane = lax.broadcasted_iota(jnp.int32, (rows, LANES), 1)
    n = length // LANES
    order = range(n - 1, -1, -1) if reverse else range(n)
    out = [None] * n
    carry = jnp.zeros((rows, 1), F32)
    for c in order:
        v = x[:, c * LANES:(c + 1) * LANES]
        d = 1
        while d < LANES:
            if reverse:
                v = v + jnp.where(lane < LANES - d, pltpu.roll(v, LANES - d, axis=1), 0.0)
            else:
                v = v + jnp.where(lane >= d, pltpu.roll(v, d, axis=1), 0.0)
            d *= 2
        v = v + carry
        carry = v[:, 0:1] if reverse else v[:, LANES - 1:LANES]
        out[c] = v
    return jnp.concatenate(out, axis=1)


def _ffn_ln_kernel(x_ref, wu_ref, wd_ref, g_ref, b_ref, o_ref, *, d_ff, chunk, alpha):
    x = x_ref[...]
    xb = x.astype(BF16)
    acc = jnp.zeros(x.shape, F32)
    for c in range(d_ff // chunk):
        lo = c * chunk
        gate = _dot(xb, wu_ref[:, lo:lo + chunk])
        up = _dot(xb, wu_ref[:, d_ff + lo:d_ff + lo + chunk])
        act = (gate * jax.nn.sigmoid(gate)) * up
        acc = acc + _dot(act.astype(BF16), wd_ref[lo:lo + chunk, :])
    o_ref[...] = _layer_norm(alpha * x + 0.5 * acc, g_ref[...], b_ref[...])


def _ffn_ln(x, w_up, w_down, g, b, *, alpha, tm):
    n, d = x.shape
    d_ff = w_down.shape[0]
    kern = functools.partial(_ffn_ln_kernel, d_ff=d_ff, chunk=256, alpha=alpha)
    return pl.pallas_call(
        kern,
        grid=(n // tm,),
        in_specs=[pl.BlockSpec((tm, d), lambda i: (i, 0)),
                  _const_spec(w_up.shape), _const_spec(w_down.shape),
                  _const_spec((1, d)), _const_spec((1, d))],
        out_specs=pl.BlockSpec((tm, d), lambda i: (i, 0)),
        out_shape=jax.ShapeDtypeStruct((n, d), F32),
        compiler_params=_cparams("parallel"),
        name="ffn_ln",
    )(x, w_up, w_down, g.reshape(1, d), b.reshape(1, d))


def _in_proj_kernel(x_ref, wrow_ref, wt_ref, wu_ref, wft_ref, bf_ref,
                    row_ref, qt_ref, kt_ref, vt_ref, vtb_ref, u_ref, lft_ref,
                    *, att, scale, row_scale):
    xb = x_ref[...].astype(BF16)
    row_ref[...] = (_dot(xb, wrow_ref[...]) * row_scale).astype(BF16)
    qkv_t = _dot_nt(wt_ref[...], xb)
    qt_ref[...] = (qkv_t[:att] * scale).astype(BF16)
    kt_ref[...] = qkv_t[att:2 * att]
    v_t = qkv_t[2 * att:]
    vt_ref[...] = v_t
    vtb_ref[...] = v_t.astype(BF16)
    u_ref[...] = _dot(xb, wu_ref[...])
    f = _dot_nt(wft_ref[...], xb) + bf_ref[...]
    lft_ref[...] = jnp.minimum(f, 0.0) - jnp.log1p(jnp.exp(-jnp.abs(f)))


def _in_proj(x, w_row, w_t, w_u, w_ft, b_f, *, tm, groups, scale, row_scale):
    n, d = x.shape
    att = w_t.shape[0] // 3
    pw = w_u.shape[1]
    nh = w_ft.shape[0]
    rows = n // groups
    tiles = rows // tm
    row = lambda w: pl.BlockSpec((tm, w), lambda i: (i, 0))
    col = lambda h: pl.BlockSpec((h, tm), lambda i: (0, i))
    grouped = pl.BlockSpec((None, att, tm), lambda i: (i // tiles, 0, i % tiles))
    kern = functools.partial(_in_proj_kernel, att=att, scale=scale, row_scale=row_scale)
    return pl.pallas_call(
        kern,
        grid=(n // tm,),
        in_specs=[row(d), _const_spec(w_row.shape), _const_spec(w_t.shape),
                  _const_spec(w_u.shape), _const_spec(w_ft.shape), _const_spec((nh, 1))],
        out_specs=[row(att), col(att), grouped, grouped, col(att), row(pw), col(nh)],
        out_shape=[jax.ShapeDtypeStruct((n, att), BF16),
                   jax.ShapeDtypeStruct((att, n), BF16),
                   jax.ShapeDtypeStruct((groups, att, rows), F32),
                   jax.ShapeDtypeStruct((groups, att, rows), F32),
                   jax.ShapeDtypeStruct((att, n), BF16),
                   jax.ShapeDtypeStruct((n, pw), F32),
                   jax.ShapeDtypeStruct((nh, n), F32)],
        compiler_params=_cparams("parallel"),
        name="in_proj",
    )(x, w_row, w_t, w_u, w_ft, b_f.reshape(nh, 1))


def _attn_prompt_kernel(lf_ref, qt_ref, k_ref, vt_ref, o_ref, nb_scr, *, tile, head_dim):
    seq = k_ref.shape[0]
    c = _lane_cumsum(lf_ref[...])
    for h in range(2):
        nb_scr[h] = -(jnp.broadcast_to(c[h:h + 1, :], (LANES, seq)).T)

    sub = lax.broadcasted_iota(jnp.int32, (2 * head_dim, tile), 0)
    key_i = lax.broadcasted_iota(jnp.int32, (tile, tile), 0)
    qry_i = lax.broadcasted_iota(jnp.int32, (tile, tile), 1)
    reps = tile // LANES

    def q_block(qi, _):
        q0 = pl.multiple_of(qi * tile, tile)
        qt = qt_ref[:, pl.ds(q0, tile)]
        zero = jnp.zeros_like(qt)
        qt_heads = (jnp.where(sub < head_dim, qt, zero), jnp.where(sub < head_dim, zero, qt))

        def kv_block(j, carry, masked):
            k0 = pl.multiple_of(j * tile, tile)
            kb = k_ref[pl.ds(k0, tile), :]
            scores = [_dot(kb, qt_heads[h]) for h in range(2)]
            stats = []
            for h in range(2):
                m, l, _ = carry[h]
                nb = nb_scr[h, pl.ds(k0, tile), :]
                s = scores[h] + jnp.concatenate([nb] * reps, axis=1)
                if masked:
                    s = jnp.where(key_i <= qry_i, s, NEG_BIG)
                m_new = jnp.maximum(m, jnp.max(s, axis=0, keepdims=True))
                p = jnp.exp(s - m_new)
                corr = jnp.exp(m - m_new)
                l = l * corr + jnp.sum(p, axis=0, keepdims=True)
                stats.append((m_new, l, corr, p.astype(BF16)))
            out = []
            for h in range(2):
                m_new, l, corr, pb = stats[h]
                vt = vt_ref[h * head_dim:(h + 1) * head_dim, pl.ds(k0, tile)]
                acc = carry[h][2] * corr + _dot(vt, pb)
                out.append((m_new, l, acc))
            return tuple(out)

        init = tuple((jnp.full((1, tile), NEG_BIG, F32), jnp.zeros((1, tile), F32),
                      jnp.zeros((head_dim, tile), F32)) for _ in range(2))
        carry = lax.fori_loop(0, qi, lambda j, cr: kv_block(j, cr, False), init)
        (_, l0, a0), (_, l1, a1) = kv_block(qi, carry, True)
        o_t = jnp.concatenate([a0 * (1.0 / l0), a1 * (1.0 / l1)], axis=0)
        o_ref[pl.ds(q0, tile), :] = o_t.T.astype(o_ref.dtype)
        return 0

    lax.fori_loop(0, seq // tile, q_block, 0)


def _attn_prompt(qt, kb, vtb, lf4, *, batch, seq, head_dim):
    n, att = kb.shape
    w = 2 * head_dim
    kern = functools.partial(_attn_prompt_kernel, tile=ATT_TILE, head_dim=head_dim)
    return pl.pallas_call(
        kern,
        grid=(batch, att // w),
        in_specs=[pl.BlockSpec((None, 2, seq), lambda b, h: (h, 0, b)),
                  pl.BlockSpec((w, seq), lambda b, h: (h, b)),
                  pl.BlockSpec((seq, w), lambda b, h: (b, h)),
                  pl.BlockSpec((w, seq), lambda b, h: (h, b))],
        out_specs=pl.BlockSpec((seq, w), lambda b, h: (b, h)),
        out_shape=jax.ShapeDtypeStruct((n, att), BF16),
        scratch_shapes=[pltpu.VMEM((2, seq, LANES), F32)],
        compiler_params=_cparams("parallel", "parallel"),
        name="attn_prompt",
    )(lf4, qt, kb, vtb)


def _attn_sample_kernel(pt_ref, q_ref, lfn_ref, kn_ref, vn_ref, *rest, pages, heads, head_dim):
    del pt_ref
    k_refs = rest[:pages]
    v_refs = rest[pages:2 * pages]
    lf_refs = rest[2 * pages:3 * pages]
    o_ref = rest[3 * pages]
    carry_scr, m_scr, l_scr, acc_scr = rest[3 * pages + 1:]
    g = pl.program_id(1)
    nq = q_ref.shape[0]
    page = kn_ref.shape[-1]

    q = q_ref[...]
    q_heads = [q[:, h * head_dim:(h + 1) * head_dim] for h in range(heads)]

    def expand(x):
        return jnp.concatenate(
            [jnp.broadcast_to(x[h:h + 1, :], (nq, x.shape[1])) for h in range(heads)], axis=0)

    def process(krefs, vrefs, bias, mask):
        s = jnp.concatenate(
            [jnp.concatenate([_dot(q_heads[h], kr[h].astype(BF16)) for kr in krefs], axis=1)
             for h in range(heads)], axis=0) + bias
        if mask is not None:
            s = jnp.where(mask, s, NEG_BIG)
        m_old = m_scr[...]
        m_new = jnp.maximum(m_old, jnp.max(s, axis=-1, keepdims=True))
        p = jnp.exp(s - m_new)
        corr = jnp.exp(m_old - m_new)
        l_scr[...] = l_scr[...] * corr + jnp.sum(p, axis=-1, keepdims=True)
        m_scr[...] = m_new
        pb = p.astype(BF16)
        pv = []
        for h in range(heads):
            tot = None
            for j, vr in enumerate(vrefs):
                t = _dot_nt(pb[h * nq:(h + 1) * nq, j * page:(j + 1) * page], vr[h].astype(BF16))
                tot = t if tot is None else tot + t
            pv.append(tot)
        acc_scr[...] = acc_scr[...] * corr + jnp.concatenate(pv, axis=0)

    @pl.when(g == 0)
    def _():
        carry_scr[...] = jnp.zeros(carry_scr.shape, F32)
        m_scr[...] = jnp.full(m_scr.shape, NEG_BIG, F32)
        l_scr[...] = jnp.zeros(l_scr.shape, F32)
        acc_scr[...] = jnp.zeros(acc_scr.shape, F32)
        c_new = _lane_cumsum(lfn_ref[...])
        rows = lax.broadcasted_iota(jnp.int32, (heads * nq, page), 0)
        cols = lax.broadcasted_iota(jnp.int32, (heads * nq, page), 1)
        process([kn_ref], [vn_ref], -expand(c_new), cols <= rows % nq)

    carry = carry_scr[...]
    bias = [None] * pages
    for j in range(pages - 1, -1, -1):
        lf = lf_refs[j][...]
        incl = _lane_cumsum(lf, reverse=True)
        bias[j] = expand(incl - lf + carry)
        carry = carry + incl[:, 0:1]
    carry_scr[...] = carry
    process(k_refs, v_refs, jnp.concatenate(bias, axis=1), None)

    @pl.when(g == pl.num_programs(1) - 1)
    def _():
        out = acc_scr[...] * (1.0 / l_scr[...])
        o_ref[...] = jnp.concatenate(
            [out[h * nq:(h + 1) * nq, :] for h in range(heads)], axis=1).astype(o_ref.dtype)


def _attn_sample(layer, page_table, qs, lfn_t, k_new, v_new, cache_kt, cache_vt, cache_lft):
    nb, nq, att = qs.shape
    _, _, heads, head_dim, page = cache_kt.shape
    n_pages = page_table.shape[1]
    pages = PAGES_PER_STEP
    groups = n_pages // pages

    def page_spec(j, tail):
        return pl.BlockSpec((None, None) + tail,
                            lambda b, g, pt: (layer, pt[b, (groups - 1 - g) * pages + j])
                            + (0,) * len(tail))

    def per_seq(shape):
        return pl.BlockSpec((None,) + shape, lambda b, g, pt: (b,) + (0,) * len(shape))

    kern = functools.partial(_attn_sample_kernel, pages=pages, heads=heads, head_dim=head_dim)
    grid_spec = pltpu.PrefetchScalarGridSpec(
        num_scalar_prefetch=1,
        grid=(nb, groups),
        in_specs=[per_seq((nq, att)), per_seq((heads, page)),
                  per_seq((heads, head_dim, page)), per_seq((heads, head_dim, page))]
                 + [page_spec(j, (heads, head_dim, page)) for j in range(pages)] * 2
                 + [page_spec(j, (heads, page)) for j in range(pages)],
        out_specs=per_seq((nq, att)),
        scratch_shapes=[pltpu.VMEM((heads, 1), F32),
                        pltpu.VMEM((heads * nq, 1), F32), pltpu.VMEM((heads * nq, 1), F32),
                        pltpu.VMEM((heads * nq, head_dim), F32)],
    )
    return pl.pallas_call(
        kern,
        grid_spec=grid_spec,
        out_shape=jax.ShapeDtypeStruct((nb, nq, att), BF16),
        compiler_params=_cparams("parallel", "arbitrary"),
        name="attn_sample",
    )(page_table, qs, lfn_t, k_new, v_new,
      *([cache_kt] * pages), *([cache_vt] * pages), *([cache_lft] * pages))


def _window_means_minus_self(ext, u, pos, axis):
    halo = ext.shape[axis] - u.shape[axis]
    gw = u.shape[-1] // len(POOL_WINDOWS)
    outs = []
    for gi, w in enumerate(POOL_WINDOWS):
        a = ext[..., gi * gw:(gi + 1) * gw]
        d = 1
        while d < w:
            a = a + pltpu.roll(a, d, axis=axis)
            d *= 2
        a = lax.slice_in_dim(a, halo, ext.shape[axis], axis=axis)
        inv = 1.0 / jnp.minimum(pos + 1, w).astype(F32)
        outs.append(a * inv - u[..., gi * gw:(gi + 1) * gw])
    return outs


def _merge_tail(x, o_att, mixed, wg_ref, wgrp_ref, ps_ref, wba_ref, wbb_ref, wo_ref,
                g_ref, b_ref, *, alpha):
    xb = x.astype(BF16)
    d = x.shape[-1]
    gates = jax.nn.sigmoid(_dot(xb, wg_ref[...]))
    pooled = jnp.concatenate(
        [_dot(mx.astype(BF16), wgrp_ref[gi]) for gi, mx in enumerate(mixed)], axis=-1)
    pooled = pooled * ps_ref[...]
    merged = (gates[:, :d] * _dot(o_att, wba_ref[...])
              + gates[:, d:] * _dot(pooled.astype(BF16), wbb_ref[...]))
    y = alpha * x + _dot(merged.astype(BF16), wo_ref[...])
    return _layer_norm(y, g_ref[...], b_ref[...])


def _merge_prompt_kernel(x_ref, oa_ref, u_ref, uh_ref, *rest, alpha, tiles_per_seq):
    o_ref = rest[-1]
    i = pl.program_id(0)
    t = i % tiles_per_seq
    u = u_ref[...]
    tm = u.shape[0]
    halo = jnp.where(t == 0, 0.0, uh_ref[...])
    ext = jnp.concatenate([halo, u], axis=0)
    pos = t * tm + lax.broadcasted_iota(jnp.int32, (tm, 1), 0)
    mixed = _window_means_minus_self(ext, u, pos, axis=0)
    o_ref[...] = _merge_tail(x_ref[...], oa_ref[...], mixed, *rest[:-1], alpha=alpha)


def _merge_sample_kernel(x_ref, oa_ref, u_ref, st_ref, *rest, alpha, start):
    o_ref = rest[-1]
    u = u_ref[...]
    nb, ns, ch = u.shape
    ext = jnp.concatenate([st_ref[...], u], axis=1)
    pos = start + lax.broadcasted_iota(jnp.int32, (1, ns, 1), 1)
    mixed = [m.reshape(nb * ns, -1) for m in _window_means_minus_self(ext, u, pos, axis=1)]
    o_ref[...] = _merge_tail(x_ref[...], oa_ref[...], mixed, *rest[:-1], alpha=alpha)


def _merge_weight_specs(ws):
    return [_const_spec(w.shape) for w in ws]


def _merge_prompt(x, o_att, u, weights, *, alpha, tm, seq):
    n, d = x.shape
    att = o_att.shape[1]
    pw = u.shape[1]
    hb = tm // POOL_HALO
    kern = functools.partial(_merge_prompt_kernel, alpha=alpha, tiles_per_seq=seq // tm)
    return pl.pallas_call(
        kern,
        grid=(n // tm,),
        in_specs=[pl.BlockSpec((tm, d), lambda i: (i, 0)),
                  pl.BlockSpec((tm, att), lambda i: (i, 0)),
                  pl.BlockSpec((tm, pw), lambda i: (i, 0)),
                  pl.BlockSpec((POOL_HALO, pw), lambda i: (jnp.maximum(i * hb - 1, 0), 0))]
                 + _merge_weight_specs(weights),
        out_specs=pl.BlockSpec((tm, d), lambda i: (i, 0)),
        out_shape=jax.ShapeDtypeStruct((n, d), F32),
        compiler_params=_cparams("parallel"),
        name="merge_prompt",
    )(x, o_att, u, u, *weights)


def _merge_sample(x, o_att, u3, state, weights, *, alpha, start):
    n, d = x.shape
    kern = functools.partial(_merge_sample_kernel, alpha=alpha, start=start)
    return pl.pallas_call(
        kern,
        grid=(1,),
        in_specs=[_const_spec(x.shape), _const_spec(o_att.shape), _const_spec(u3.shape),
                  _const_spec(state.shape)] + _merge_weight_specs(weights),
        out_specs=_const_spec((n, d)),
        out_shape=jax.ShapeDtypeStruct((n, d), F32),
        compiler_params=_cparams("arbitrary"),
        name="merge_sample",
    )(x, o_att, u3, state, *weights)


def kernel(x_prompt, x_sample, cache_k, cache_v, cache_logf, state_pool, page_table,
           ln_g, ln_b, w_ffn_up, w_ffn_down, w_in, b_f, w_pool_grp, pool_scale,
           w_branch_a, w_branch_b, w_out):
    batch, seq, d = x_prompt.shape
    dec_b, dec_s, _ = x_sample.shape
    depth, _, page_size, heads, head_dim = cache_k.shape
    att = heads * head_dim
    pool_w = pool_scale.shape[1]
    pool_buf = state_pool.shape[2]
    n_pages = page_table.shape[1]
    past = n_pages * page_size
    alpha = (2.0 * depth) ** 0.25
    scale = head_dim ** -0.5
    np_rows = batch * seq
    ns_rows = dec_b * dec_s
    assert pool_buf == max(POOL_WINDOWS) - 1 and pool_w % (len(POOL_WINDOWS) * LANES) == 0
    assert seq % ROW_TILE == 0 and seq % ATT_TILE == 0 and n_pages % PAGES_PER_STEP == 0
    assert dec_s == SUBLANES and heads % 2 == 0 and 2 * head_dim == LANES and page_size == LANES

    cache_kt = cache_k.transpose(0, 1, 3, 4, 2)
    cache_vt = cache_v.transpose(0, 1, 3, 4, 2)
    cache_lft = cache_logf.transpose(0, 1, 3, 2)

    xp = x_prompt.reshape(np_rows, d)
    xs = x_sample.reshape(ns_rows, d)
    outs = {k: [] for k in ("kp", "vp", "fp", "bp", "ks", "vs", "fs", "bs")}

    for l in range(depth):
        wu = w_ffn_up[l].astype(BF16)
        wd = w_ffn_down[l].astype(BF16)
        wi = w_in[l]
        w_q = wi[:, :att].astype(BF16)
        w_k = wi[:, att:2 * att].astype(BF16)
        w_t = wi[:, :3 * att].T.astype(BF16)
        w_ft = wi[:, 3 * att:3 * att + heads].T.astype(BF16)
        c0 = 3 * att + heads
        w_u = wi[:, c0:c0 + pool_w].astype(BF16)
        w_g = wi[:, c0 + pool_w:].astype(BF16)
        merge_w = (w_g, w_pool_grp[l].astype(BF16), pool_scale[l].reshape(1, pool_w),
                   w_branch_a[l].astype(BF16), w_branch_b[l].astype(BF16),
                   w_out[l].astype(BF16), ln_g[l, 1].reshape(1, d), ln_b[l, 1].reshape(1, d))

        x1 = _ffn_ln(xp, wu[0], wd[0], ln_g[l, 0], ln_b[l, 0], alpha=alpha, tm=ROW_TILE)
        kb, qt, kt, vt, vtb, u, lft = _in_proj(x1, w_k, w_t, w_u, w_ft, b_f[l], tm=ROW_TILE,
                                               groups=batch, scale=scale, row_scale=1.0)
        o_att = _attn_prompt(qt, kb, vtb, lft.reshape(heads // 2, 2, np_rows),
                             batch=batch, seq=seq, head_dim=head_dim)
        x2 = _merge_prompt(x1, o_att, u, merge_w, alpha=alpha, tm=ROW_TILE, seq=seq)
        xp = _ffn_ln(x2, wu[1], wd[1], ln_g[l, 2], ln_b[l, 2], alpha=alpha, tm=ROW_TILE)
        to_out = lambda a: a.reshape(batch, heads, head_dim, seq).transpose(0, 3, 1, 2)
        outs["kp"].append(to_out(kt))
        outs["vp"].append(to_out(vt))
        outs["fp"].append(lft.reshape(heads, batch, seq).transpose(1, 2, 0))
        outs["bp"].append(u.reshape(batch, seq, pool_w)[:, seq - pool_buf:])

        y1 = _ffn_ln(xs, wu[0], wd[0], ln_g[l, 0], ln_b[l, 0], alpha=alpha, tm=ns_rows)
        qb, _, kt, vt, _, u, lft = _in_proj(y1, w_q, w_t, w_u, w_ft, b_f[l], tm=ns_rows,
                                            groups=1, scale=scale, row_scale=scale)
        lf_new = lft.reshape(heads, dec_b, dec_s).transpose(1, 0, 2)
        lfn_t = jnp.pad(lf_new, ((0, 0), (0, 0), (0, page_size - dec_s)))
        kt4 = kt.reshape(heads, head_dim, dec_b, dec_s)
        vt4 = vt.reshape(heads, head_dim, dec_b, dec_s)
        as_page = lambda a: jnp.pad(a.transpose(2, 0, 1, 3),
                                    ((0, 0), (0, 0), (0, 0), (0, page_size - dec_s)))
        o_att = _attn_sample(l, page_table, qb.reshape(dec_b, dec_s, att), lfn_t,
                             as_page(kt4), as_page(vt4), cache_kt, cache_vt, cache_lft)
        state = jnp.pad(state_pool[l], ((0, 0), (POOL_HALO - pool_buf, 0), (0, 0)))
        u3 = u.reshape(dec_b, dec_s, pool_w)
        y2 = _merge_sample(y1, o_att.reshape(ns_rows, att), u3, state, merge_w,
                           alpha=alpha, start=past)
        xs = _ffn_ln(y2, wu[1], wd[1], ln_g[l, 2], ln_b[l, 2], alpha=alpha, tm=ns_rows)
        outs["ks"].append(kt4.transpose(2, 3, 0, 1))
        outs["vs"].append(vt4.transpose(2, 3, 0, 1))
        outs["fs"].append(lf_new.transpose(0, 2, 1))
        outs["bs"].append(jnp.concatenate([state_pool[l], u3], axis=1)[:, dec_s:])

    st = {k: jnp.stack(v) for k, v in outs.items()}
    return (xp.reshape(batch, seq, d), xs.reshape(dec_b, dec_s, d),
            st["kp"], st["vp"], st["fp"], st["bp"], st["ks"], st["vs"], st["fs"], st["bs"])
```

```python
import functools

import jax
import jax.numpy as jnp
from jax import lax
from jax.experimental import pallas as pl
from jax.experimental.pallas import tpu as pltpu

F32 = jnp.float32
BF16 = jnp.bfloat16

LANES = 128
SUBLANES = 8
VMEM_LIMIT_BYTES = 56 * 1024 * 1024

POOL_WINDOWS = (2, 4, 8, 16)
POOL_HALO = 16
LN_EPS = 1e-5
NEG_BIG = -1e30

ROW_TILE = 512
ATT_TILE = 512
PAGES_PER_STEP = 16


def _cparams(*sem):
    return pltpu.CompilerParams(dimension_semantics=sem, vmem_limit_bytes=VMEM_LIMIT_BYTES)


def _const_spec(shape):
    nd = len(shape)
    return pl.BlockSpec(shape, lambda *_: (0,) * nd)


def _layer_norm(y, g, b):
    mu = jnp.mean(y, axis=-1, keepdims=True)
    d = y - mu
    var = jnp.mean(d * d, axis=-1, keepdims=True)
    return d * lax.rsqrt(var + LN_EPS) * g + b


def _dot(a, b):
    return jnp.dot(a, b, preferred_element_type=F32)


def _dot_nt(a, b):
    return lax.dot_general(a, b, (((1,), (1,)), ((), ())), preferred_element_type=F32)


def _lane_cumsum(x, reverse=False):
    rows, length = x.shape
    lane = lax.broadcasted_iota(jnp.int32, (rows, LANES), 1)
    n = length // LANES
    order = range(n - 1, -1, -1) if reverse else range(n)
    out = [None] * n
    carry = jnp.zeros((rows, 1), F32)
    for c in order:
        v = x[:, c * LANES:(c + 1) * LANES]
        d = 1
        while d < LANES:
            if reverse:
                v = v + jnp.where(lane < LANES - d, pltpu.roll(v, LANES - d, axis=1), 0.0)
            else:
                v = v + jnp.where(lane >= d, pltpu.roll(v, d, axis=1), 0.0)
            d *= 2
        v = v + carry
        carry = v[:, 0:1] if reverse else v[:, LANES - 1:LANES]
        out[c] = v
    return jnp.concatenate(out, axis=1)


FFN_CHUNK = 256


def _ffn_ln_body(x, wu_ref, wd_ref, g_ref, b_ref, *, alpha):
    d_ff = wd_ref.shape[0]
    xb = x.astype(BF16)
    acc = jnp.zeros(x.shape, F32)
    for lo in range(0, d_ff, FFN_CHUNK):
        gate = _dot(xb, wu_ref[:, lo:lo + FFN_CHUNK])
        up = _dot(xb, wu_ref[:, d_ff + lo:d_ff + lo + FFN_CHUNK])
        act = (gate * jax.nn.sigmoid(gate)) * up
        acc = acc + _dot(act.astype(BF16), wd_ref[lo:lo + FFN_CHUNK, :])
    return _layer_norm(alpha * x + 0.5 * acc, g_ref[...], b_ref[...])


def _ffn_in_proj_kernel(x_ref, wu_ref, wd_ref, g_ref, b_ref, wrow_ref, wt_ref, wpu_ref, wft_ref,
                        bf_ref, x1_ref, row_ref, qt_ref, kt_ref, vt_ref, vtb_ref, u_ref, lft_ref,
                        *, att, alpha, scale, row_scale):
    x1 = _ffn_ln_body(x_ref[...], wu_ref, wd_ref, g_ref, b_ref, alpha=alpha)
    x1_ref[...] = x1
    xb = x1.astype(BF16)
    row_ref[...] = (_dot(xb, wrow_ref[...]) * row_scale).astype(BF16)
    qkv_t = _dot_nt(wt_ref[...], xb)
    qt_ref[...] = (qkv_t[:att] * scale).astype(BF16)
    kt_ref[...] = qkv_t[att:2 * att]
    v_t = qkv_t[2 * att:]
    vt_ref[...] = v_t
    vtb_ref[...] = v_t.astype(BF16)
    u_ref[...] = _dot(xb, wpu_ref[...])
    f = _dot_nt(wft_ref[...], xb) + bf_ref[...]
    lft_ref[...] = jnp.minimum(f, 0.0) - jnp.log1p(jnp.exp(-jnp.abs(f)))


def _ffn_in_proj(x, ffn_w, w_row, w_t, w_u, w_ft, b_f, *, tm, groups, alpha, scale, row_scale):
    n, d = x.shape
    att = w_t.shape[0] // 3
    pw = w_u.shape[1]
    nh = w_ft.shape[0]
    rows = n // groups
    tiles = rows // tm
    row = lambda w: pl.BlockSpec((tm, w), lambda i: (i, 0))
    col = lambda h: pl.BlockSpec((h, tm), lambda i: (0, i))
    grouped = pl.BlockSpec((None, att, tm), lambda i: (i // tiles, 0, i % tiles))
    weights = (*ffn_w, w_row, w_t, w_u, w_ft, b_f.reshape(nh, 1))
    kern = functools.partial(_ffn_in_proj_kernel, att=att, alpha=alpha, scale=scale,
                             row_scale=row_scale)
    return pl.pallas_call(
        kern,
        grid=(n // tm,),
        in_specs=[row(d)] + [_const_spec(w.shape) for w in weights],
        out_specs=[row(d), row(att), col(att), grouped, grouped, col(att), row(pw), col(nh)],
        out_shape=[jax.ShapeDtypeStruct((n, d), F32),
                   jax.ShapeDtypeStruct((n, att), BF16),
                   jax.ShapeDtypeStruct((att, n), BF16),
                   jax.ShapeDtypeStruct((groups, att, rows), F32),
                   jax.ShapeDtypeStruct((groups, att, rows), F32),
                   jax.ShapeDtypeStruct((att, n), BF16),
                   jax.ShapeDtypeStruct((n, pw), F32),
                   jax.ShapeDtypeStruct((nh, n), F32)],
        compiler_params=_cparams("parallel"),
        name="ffn_in_proj",
    )(x, *weights)


def _attn_prompt_kernel(lf_ref, qt_ref, k_ref, vt_ref, o_ref, nb_scr, *, tile, head_dim):
    seq = k_ref.shape[0]
    c = _lane_cumsum(lf_ref[...])
    for h in range(2):
        nb_scr[h] = -(jnp.broadcast_to(c[h:h + 1, :], (LANES, seq)).T)

    sub = lax.broadcasted_iota(jnp.int32, (2 * head_dim, tile), 0)
    key_i = lax.broadcasted_iota(jnp.int32, (tile, tile), 0)
    qry_i = lax.broadcasted_iota(jnp.int32, (tile, tile), 1)
    reps = tile // LANES

    def q_block(qi, _):
        q0 = pl.multiple_of(qi * tile, tile)
        qt = qt_ref[:, pl.ds(q0, tile)]
        zero = jnp.zeros_like(qt)
        qt_heads = (jnp.where(sub < head_dim, qt, zero), jnp.where(sub < head_dim, zero, qt))

        def scores(j):
            k0 = pl.multiple_of(j * tile, tile)
            kb = k_ref[pl.ds(k0, tile), :]
            return tuple(
                _dot(kb, qt_heads[h])
                + jnp.concatenate([nb_scr[h, pl.ds(k0, tile), :]] * reps, axis=1)
                for h in range(2))

        def absorb(j, s_pair, state):
            k0 = pl.multiple_of(j * tile, tile)
            stats = []
            for h in range(2):
                m, l, _ = state[h]
                s = s_pair[h]
                m_new = jnp.maximum(m, jnp.max(s, axis=0, keepdims=True))
                p = jnp.exp(s - m_new)
                corr = jnp.exp(m - m_new)
                l = l * corr + jnp.sum(p, axis=0, keepdims=True)
                stats.append((m_new, l, corr, p.astype(BF16)))
            out = []
            for h in range(2):
                m_new, l, corr, pb = stats[h]
                vt = vt_ref[h * head_dim:(h + 1) * head_dim, pl.ds(k0, tile)]
                acc = state[h][2] * corr + _dot(vt, pb)
                out.append((m_new, l, acc))
            return tuple(out)

        init = tuple((jnp.full((1, tile), NEG_BIG, F32), jnp.zeros((1, tile), F32),
                      jnp.zeros((head_dim, tile), F32)) for _ in range(2))
        state = lax.fori_loop(0, qi, lambda j, st: absorb(j, scores(j), st), init)
        diag = tuple(jnp.where(key_i <= qry_i, s, NEG_BIG) for s in scores(qi))
        (_, l0, a0), (_, l1, a1) = absorb(qi, diag, state)
        o_t = jnp.concatenate([a0 * (1.0 / l0), a1 * (1.0 / l1)], axis=0)
        o_ref[pl.ds(q0, tile), :] = o_t.T.astype(o_ref.dtype)
        return 0

    lax.fori_loop(0, seq // tile, q_block, 0)


def _attn_prompt(qt, kb, vtb, lf4, *, batch, seq, head_dim):
    n, att = kb.shape
    w = 2 * head_dim
    kern = functools.partial(_attn_prompt_kernel, tile=ATT_TILE, head_dim=head_dim)
    return pl.pallas_call(
        kern,
        grid=(batch, att // w),
        in_specs=[pl.BlockSpec((None, 2, seq), lambda b, h: (h, 0, b)),
                  pl.BlockSpec((w, seq), lambda b, h: (h, b)),
                  pl.BlockSpec((seq, w), lambda b, h: (b, h)),
                  pl.BlockSpec((w, seq), lambda b, h: (h, b))],
        out_specs=pl.BlockSpec((seq, w), lambda b, h: (b, h)),
        out_shape=jax.ShapeDtypeStruct((n, att), BF16),
        scratch_shapes=[pltpu.VMEM((2, seq, LANES), F32)],
        compiler_params=_cparams("parallel", "parallel"),
        name="attn_prompt",
    )(lf4, qt, kb, vtb)


def _attn_sample_kernel(pt_ref, q_ref, lfn_ref, kn_ref, vn_ref, *rest, pages, heads, head_dim):
    del pt_ref
    k_refs = rest[:pages]
    v_refs = rest[pages:2 * pages]
    lf_refs = rest[2 * pages:3 * pages]
    o_ref = rest[3 * pages]
    carry_scr, m_scr, l_scr, acc_scr = rest[3 * pages + 1:]
    g = pl.program_id(1)
    nq, att = q_ref.shape
    page = kn_ref.shape[-1]
    rows = heads * nq

    q_rep = jnp.concatenate([q_ref[...].astype(F32)] * heads, axis=0)
    r_i = lax.broadcasted_iota(jnp.int32, (rows, att), 0)
    c_i = lax.broadcasted_iota(jnp.int32, (rows, att), 1)
    q_bd = jnp.where(r_i // nq == c_i // head_dim, q_rep, 0.0).astype(BF16)

    def expand(x):
        return jnp.concatenate(
            [jnp.broadcast_to(x[h:h + 1, :], (nq, x.shape[1])) for h in range(heads)], axis=0)

    def process(krefs, vrefs, bias_fn, mask):
        s = jnp.concatenate([_dot(q_bd, kr[...].astype(BF16)) for kr in krefs], axis=1)
        s = s + bias_fn()
        if mask is not None:
            s = jnp.where(mask, s, NEG_BIG)
        m_old = m_scr[...]
        m_new = jnp.maximum(m_old, jnp.max(s, axis=-1, keepdims=True))
        p = jnp.exp(s - m_new)
        corr = jnp.exp(m_old - m_new)
        l_scr[...] = l_scr[...] * corr + jnp.sum(p, axis=-1, keepdims=True)
        m_scr[...] = m_new
        pb = p.astype(BF16)
        pv = None
        for j, vr in enumerate(vrefs):
            t = _dot_nt(pb[:, j * page:(j + 1) * page], vr[...].astype(BF16))
            pv = t if pv is None else pv + t
        acc_scr[...] = acc_scr[...] * corr + pv

    @pl.when(g == 0)
    def _():
        carry_scr[...] = jnp.zeros(carry_scr.shape, F32)
        m_scr[...] = jnp.full(m_scr.shape, NEG_BIG, F32)
        l_scr[...] = jnp.zeros(l_scr.shape, F32)
        acc_scr[...] = jnp.zeros(acc_scr.shape, F32)
        c_new = _lane_cumsum(lfn_ref[...])
        row = lax.broadcasted_iota(jnp.int32, (rows, page), 0)
        col = lax.broadcasted_iota(jnp.int32, (rows, page), 1)
        process([kn_ref], [vn_ref], lambda: -expand(c_new), col <= row % nq)

    def past_bias():
        carry = carry_scr[...]
        bias = [None] * pages
        for j in range(pages - 1, -1, -1):
            lf = lf_refs[j][...]
            incl = _lane_cumsum(lf, reverse=True)
            bias[j] = expand(incl - lf + carry)
            carry = carry + incl[:, 0:1]
        carry_scr[...] = carry
        return jnp.concatenate(bias, axis=1)

    process(k_refs, v_refs, past_bias, None)

    @pl.when(g == pl.num_programs(1) - 1)
    def _():
        out = acc_scr[...] * (1.0 / l_scr[...])
        o_ref[...] = jnp.concatenate(
            [out[h * nq:(h + 1) * nq, h * head_dim:(h + 1) * head_dim] for h in range(heads)],
            axis=1).astype(o_ref.dtype)


def _attn_sample(layer, page_table, qs, lfn_t, k_new, v_new, cache_kt, cache_vt, cache_lft):
    nb, nq, att = qs.shape
    heads, page = cache_lft.shape[2:]
    head_dim = att // heads
    n_pages = page_table.shape[1]
    pages = PAGES_PER_STEP
    groups = n_pages // pages

    def page_spec(j, tail):
        return pl.BlockSpec((None, None) + tail,
                            lambda b, g, pt: (layer, pt[b, (groups - 1 - g) * pages + j])
                            + (0,) * len(tail))

    def per_seq(shape):
        return pl.BlockSpec((None,) + shape, lambda b, g, pt: (b,) + (0,) * len(shape))

    kern = functools.partial(_attn_sample_kernel, pages=pages, heads=heads, head_dim=head_dim)
    grid_spec = pltpu.PrefetchScalarGridSpec(
        num_scalar_prefetch=1,
        grid=(nb, groups),
        in_specs=[per_seq((nq, att)), per_seq((heads, page)),
                  per_seq((att, page)), per_seq((att, page))]
                 + [page_spec(j, (att, page)) for j in range(pages)] * 2
                 + [page_spec(j, (heads, page)) for j in range(pages)],
        out_specs=per_seq((nq, att)),
        scratch_shapes=[pltpu.VMEM((heads, 1), F32),
                        pltpu.VMEM((heads * nq, 1), F32), pltpu.VMEM((heads * nq, 1), F32),
                        pltpu.VMEM((heads * nq, att), F32)],
    )
    return pl.pallas_call(
        kern,
        grid_spec=grid_spec,
        out_shape=jax.ShapeDtypeStruct((nb, nq, att), BF16),
        compiler_params=_cparams("parallel", "arbitrary"),
        name="attn_sample",
    )(page_table, qs, lfn_t, k_new, v_new,
      *([cache_kt] * pages), *([cache_vt] * pages), *([cache_lft] * pages))


def _window_means_minus_self(ext, u, pos, axis):
    halo = ext.shape[axis] - u.shape[axis]
    gw = u.shape[-1] // len(POOL_WINDOWS)
    outs = []
    for gi, w in enumerate(POOL_WINDOWS):
        a = ext[..., gi * gw:(gi + 1) * gw]
        d = 1
        while d < w:
            a = a + pltpu.roll(a, d, axis=axis)
            d *= 2
        a = lax.slice_in_dim(a, halo, ext.shape[axis], axis=axis)
        inv = 1.0 / jnp.minimum(pos + 1, w).astype(F32)
        outs.append(a * inv - u[..., gi * gw:(gi + 1) * gw])
    return outs


def _merge_tail(x, o_att, mixed, wg_ref, wgrp_ref, ps_ref, wba_ref, wbb_ref, wo_ref,
                g_ref, b_ref, *, alpha):
    xb = x.astype(BF16)
    d = x.shape[-1]
    gates = jax.nn.sigmoid(_dot(xb, wg_ref[...]))
    pooled = jnp.concatenate(
        [_dot(mx.astype(BF16), wgrp_ref[gi]) for gi, mx in enumerate(mixed)], axis=-1)
    pooled = pooled * ps_ref[...]
    merged = (gates[:, :d] * _dot(o_att, wba_ref[...])
              + gates[:, d:] * _dot(pooled.astype(BF16), wbb_ref[...]))
    y = alpha * x + _dot(merged.astype(BF16), wo_ref[...])
    return _layer_norm(y, g_ref[...], b_ref[...])


N_MERGE_WEIGHTS = 8


def _merge_ffn(x, o_att, mixed, weight_refs, *, alpha):
    x2 = _merge_tail(x, o_att, mixed, *weight_refs[:N_MERGE_WEIGHTS], alpha=alpha)
    return _ffn_ln_body(x2, *weight_refs[N_MERGE_WEIGHTS:], alpha=alpha)


def _merge_prompt_kernel(x_ref, oa_ref, u_ref, uh_ref, *rest, alpha, tiles_per_seq):
    o_ref = rest[-1]
    i = pl.program_id(0)
    t = i % tiles_per_seq
    u = u_ref[...]
    tm = u.shape[0]
    halo = jnp.where(t == 0, 0.0, uh_ref[...])
    ext = jnp.concatenate([halo, u], axis=0)
    pos = t * tm + lax.broadcasted_iota(jnp.int32, (tm, 1), 0)
    mixed = _window_means_minus_self(ext, u, pos, axis=0)
    o_ref[...] = _merge_ffn(x_ref[...], oa_ref[...], mixed, rest[:-1], alpha=alpha)


def _merge_sample_kernel(x_ref, oa_ref, u_ref, st_ref, *rest, alpha, start):
    o_ref = rest[-1]
    u = u_ref[...]
    nb, ns, ch = u.shape
    ext = jnp.concatenate([st_ref[...], u], axis=1)
    pos = start + lax.broadcasted_iota(jnp.int32, (1, ns, 1), 1)
    mixed = [m.reshape(nb * ns, -1) for m in _window_means_minus_self(ext, u, pos, axis=1)]
    o_ref[...] = _merge_ffn(x_ref[...], oa_ref[...], mixed, rest[:-1], alpha=alpha)


def _merge_weight_specs(ws):
    return [_const_spec(w.shape) for w in ws]


def _merge_prompt(x, o_att, u, weights, *, alpha, tm, seq):
    n, d = x.shape
    att = o_att.shape[1]
    pw = u.shape[1]
    hb = tm // POOL_HALO
    kern = functools.partial(_merge_prompt_kernel, alpha=alpha, tiles_per_seq=seq // tm)
    return pl.pallas_call(
        kern,
        grid=(n // tm,),
        in_specs=[pl.BlockSpec((tm, d), lambda i: (i, 0)),
                  pl.BlockSpec((tm, att), lambda i: (i, 0)),
                  pl.BlockSpec((tm, pw), lambda i: (i, 0)),
                  pl.BlockSpec((POOL_HALO, pw), lambda i: (jnp.maximum(i * hb - 1, 0), 0))]
                 + _merge_weight_specs(weights),
        out_specs=pl.BlockSpec((tm, d), lambda i: (i, 0)),
        out_shape=jax.ShapeDtypeStruct((n, d), F32),
        compiler_params=_cparams("parallel"),
        name="merge_prompt",
    )(x, o_att, u, u, *weights)


def _merge_sample(x, o_att, u3, state, weights, *, alpha, start):
    n, d = x.shape
    kern = functools.partial(_merge_sample_kernel, alpha=alpha, start=start)
    return pl.pallas_call(
        kern,
        grid=(1,),
        in_specs=[_const_spec(x.shape), _const_spec(o_att.shape), _const_spec(u3.shape),
                  _const_spec(state.shape)] + _merge_weight_specs(weights),
        out_specs=_const_spec((n, d)),
        out_shape=jax.ShapeDtypeStruct((n, d), F32),
        compiler_params=_cparams("arbitrary"),
        name="merge_sample",
    )(x, o_att, u3, state, *weights)


def kernel(x_prompt, x_sample, cache_k, cache_v, cache_logf, state_pool, page_table,
           ln_g, ln_b, w_ffn_up, w_ffn_down, w_in, b_f, w_pool_grp, pool_scale,
           w_branch_a, w_branch_b, w_out):
    batch, seq, d = x_prompt.shape
    dec_b, dec_s, _ = x_sample.shape
    depth, _, page_size, heads, head_dim = cache_k.shape
    att = heads * head_dim
    pool_w = pool_scale.shape[1]
    pool_buf = state_pool.shape[2]
    n_pages = page_table.shape[1]
    past = n_pages * page_size
    alpha = (2.0 * depth) ** 0.25
    scale = head_dim ** -0.5
    np_rows = batch * seq
    ns_rows = dec_b * dec_s
    assert pool_buf == max(POOL_WINDOWS) - 1 and pool_w % (len(POOL_WINDOWS) * LANES) == 0
    assert seq % ROW_TILE == 0 and seq % ATT_TILE == 0 and n_pages % PAGES_PER_STEP == 0
    assert dec_s == SUBLANES and heads % 2 == 0 and 2 * head_dim == LANES and page_size == LANES

    pool_pages = cache_k.shape[1]
    cache_kt = cache_k.transpose(0, 1, 3, 4, 2).reshape(depth, pool_pages, att, page_size)
    cache_vt = cache_v.transpose(0, 1, 3, 4, 2).reshape(depth, pool_pages, att, page_size)
    cache_lft = cache_logf.transpose(0, 1, 3, 2)

    xp = x_prompt.reshape(np_rows, d)
    xs = x_sample.reshape(ns_rows, d)
    outs = {k: [] for k in ("kp", "vp", "fp", "bp", "ks", "vs", "fs", "bs")}

    for l in range(depth):
        wu = w_ffn_up[l].astype(BF16)
        wd = w_ffn_down[l].astype(BF16)
        wi = w_in[l]
        w_q = wi[:, :att].astype(BF16)
        w_k = wi[:, att:2 * att].astype(BF16)
        w_t = wi[:, :3 * att].T.astype(BF16)
        w_ft = wi[:, 3 * att:3 * att + heads].T.astype(BF16)
        c0 = 3 * att + heads
        w_u = wi[:, c0:c0 + pool_w].astype(BF16)
        w_g = wi[:, c0 + pool_w:].astype(BF16)
        vec = lambda a: a.reshape(1, d)
        ffn0_w = (wu[0], wd[0], vec(ln_g[l, 0]), vec(ln_b[l, 0]))
        merge_w = (w_g, w_pool_grp[l].astype(BF16), pool_scale[l].reshape(1, pool_w),
                   w_branch_a[l].astype(BF16), w_branch_b[l].astype(BF16),
                   w_out[l].astype(BF16), vec(ln_g[l, 1]), vec(ln_b[l, 1]),
                   wu[1], wd[1], vec(ln_g[l, 2]), vec(ln_b[l, 2]))
        assert len(merge_w) == N_MERGE_WEIGHTS + 4

        x1, kb, qt, kt, vt, vtb, u, lft = _ffn_in_proj(
            xp, ffn0_w, w_k, w_t, w_u, w_ft, b_f[l], tm=ROW_TILE, groups=batch,
            alpha=alpha, scale=scale, row_scale=1.0)
        o_att = _attn_prompt(qt, kb, vtb, lft.reshape(heads // 2, 2, np_rows),
                             batch=batch, seq=seq, head_dim=head_dim)
        xp = _merge_prompt(x1, o_att, u, merge_w, alpha=alpha, tm=ROW_TILE, seq=seq)
        to_out = lambda a: a.reshape(batch, heads, head_dim, seq).transpose(0, 3, 1, 2)
        outs["kp"].append(to_out(kt))
        outs["vp"].append(to_out(vt))
        outs["fp"].append(lft.reshape(heads, batch, seq).transpose(1, 2, 0))
        outs["bp"].append(u.reshape(batch, seq, pool_w)[:, seq - pool_buf:])

        y1, qb, _, kt, vt, _, u, lft = _ffn_in_proj(
            xs, ffn0_w, w_q, w_t, w_u, w_ft, b_f[l], tm=ns_rows, groups=1,
            alpha=alpha, scale=scale, row_scale=scale)
        lf_new = lft.reshape(heads, dec_b, dec_s).transpose(1, 0, 2)
        lfn_t = jnp.pad(lf_new, ((0, 0), (0, 0), (0, page_size - dec_s)))
        kt4 = kt.reshape(heads, head_dim, dec_b, dec_s)
        vt4 = vt.reshape(heads, head_dim, dec_b, dec_s)
        as_page = lambda a: jnp.pad(a.reshape(att, dec_b, dec_s).transpose(1, 0, 2),
                                    ((0, 0), (0, 0), (0, page_size - dec_s)))
        o_att = _attn_sample(l, page_table, qb.reshape(dec_b, dec_s, att), lfn_t,
                             as_page(kt), as_page(vt), cache_kt, cache_vt, cache_lft)
        state = jnp.pad(state_pool[l], ((0, 0), (POOL_HALO - pool_buf, 0), (0, 0)))
        u3 = u.reshape(dec_b, dec_s, pool_w)
        xs = _merge_sample(y1, o_att.reshape(ns_rows, att), u3, state, merge_w,
                           alpha=alpha, start=past)
        outs["ks"].append(kt4.transpose(2, 3, 0, 1))
        outs["vs"].append(vt4.transpose(2, 3, 0, 1))
        outs["fs"].append(lf_new.transpose(0, 2, 1))
        outs["bs"].append(jnp.concatenate([state_pool[l], u3], axis=1)[:, dec_s:])

    st = {k: jnp.stack(v) for k, v in outs.items()}
    return (xp.reshape(batch, seq, d), xs.reshape(dec_b, dec_s, d),
            st["kp"], st["vp"], st["fp"], st["bp"], st["ks"], st["vs"], st["fs"], st["bs"])
```

```python
import functools

import jax
import jax.numpy as jnp
from jax import lax
from jax.experimental import pallas as pl
from jax.experimental.pallas import tpu as pltpu

F32 = jnp.float32
BF16 = jnp.bfloat16

LANES = 128
SUBLANES = 8
VMEM_LIMIT_BYTES = 56 * 1024 * 1024

POOL_WINDOWS = (2, 4, 8, 16)
POOL_HALO = 16
LN_EPS = 1e-5
NEG_BIG = -1e30

ROW_TILE = 512
ATT_TILE = 512
ONES_ROWS = 16
LOG2E = 1.4426950408889634
PAGES_PER_STEP = 16


def _cparams(*sem):
    return pltpu.CompilerParams(dimension_semantics=sem, vmem_limit_bytes=VMEM_LIMIT_BYTES)


def _const_spec(shape):
    nd = len(shape)
    return pl.BlockSpec(shape, lambda *_: (0,) * nd)


def _layer_norm(y, g, b):
    mu = jnp.mean(y, axis=-1, keepdims=True)
    d = y - mu
    var = jnp.mean(d * d, axis=-1, keepdims=True)
    return d * lax.rsqrt(var + LN_EPS) * g + b


def _dot(a, b):
    return jnp.dot(a, b, preferred_element_type=F32)


def _dot_nt(a, b):
    return lax.dot_general(a, b, (((1,), (1,)), ((), ())), preferred_element_type=F32)


def _lane_cumsum(x, reverse=False):
    rows, length = x.shape
    lane = lax.broadcasted_iota(jnp.int32, (rows, LANES), 1)
    n = length // LANES
    order = range(n - 1, -1, -1) if reverse else range(n)
    out = [None] * n
    carry = jnp.zeros((rows, 1), F32)
    for c in order:
        v = x[:, c * LANES:(c + 1) * LANES]
        d = 1
        while d < LANES:
            if reverse:
                v = v + jnp.where(lane < LANES - d, pltpu.roll(v, LANES - d, axis=1), 0.0)
            else:
                v = v + jnp.where(lane >= d, pltpu.roll(v, d, axis=1), 0.0)
            d *= 2
        v = v + carry
        carry = v[:, 0:1] if reverse else v[:, LANES - 1:LANES]
        out[c] = v
    return jnp.concatenate(out, axis=1)


FFN_CHUNK = 256


def _ffn_ln_body(x, wu_ref, wd_ref, g_ref, b_ref, *, alpha):
    d_ff = wd_ref.shape[0]
    xb = x.astype(BF16)
    acc = jnp.zeros(x.shape, F32)
    for lo in range(0, d_ff, FFN_CHUNK):
        gate = _dot(xb, wu_ref[:, lo:lo + FFN_CHUNK])
        up = _dot(xb, wu_ref[:, d_ff + lo:d_ff + lo + FFN_CHUNK])
        act = (gate * jax.nn.sigmoid(gate)) * up
        acc = acc + _dot(act.astype(BF16), wd_ref[lo:lo + FFN_CHUNK, :])
    return _layer_norm(alpha * x + 0.5 * acc, g_ref[...], b_ref[...])


def _ffn_in_proj_kernel(x_ref, wu_ref, wd_ref, g_ref, b_ref, wrow_ref, wt_ref, wpu_ref, wft_ref,
                        bf_ref, x1_ref, row_ref, qt_ref, kt_ref, vt_ref, vtb_ref, u_ref, lft_ref,
                        *, att, alpha, scale, row_scale):
    x1 = _ffn_ln_body(x_ref[...], wu_ref, wd_ref, g_ref, b_ref, alpha=alpha)
    x1_ref[...] = x1
    xb = x1.astype(BF16)
    row_ref[...] = (_dot(xb, wrow_ref[...]) * row_scale).astype(BF16)
    qkv_t = _dot_nt(wt_ref[...], xb)
    qt_ref[...] = (qkv_t[:att] * scale).astype(BF16)
    kt_ref[...] = qkv_t[att:2 * att]
    v_t = qkv_t[2 * att:]
    vt_ref[...] = v_t
    vtb_ref[...] = v_t.astype(BF16)
    u_ref[...] = _dot(xb, wpu_ref[...])
    f = _dot_nt(wft_ref[...], xb) + bf_ref[...]
    lft_ref[...] = jnp.minimum(f, 0.0) - jnp.log1p(jnp.exp(-jnp.abs(f)))


def _ffn_in_proj(x, ffn_w, w_row, w_t, w_u, w_ft, b_f, *, tm, groups, alpha, scale, row_scale):
    n, d = x.shape
    att = w_t.shape[0] // 3
    pw = w_u.shape[1]
    nh = w_ft.shape[0]
    rows = n // groups
    tiles = rows // tm
    row = lambda w: pl.BlockSpec((tm, w), lambda i: (i, 0))
    col = lambda h: pl.BlockSpec((h, tm), lambda i: (0, i))
    grouped = pl.BlockSpec((None, att, tm), lambda i: (i // tiles, 0, i % tiles))
    weights = (*ffn_w, w_row, w_t, w_u, w_ft, b_f.reshape(nh, 1))
    kern = functools.partial(_ffn_in_proj_kernel, att=att, alpha=alpha, scale=scale,
                             row_scale=row_scale)
    return pl.pallas_call(
        kern,
        grid=(n // tm,),
        in_specs=[row(d)] + [_const_spec(w.shape) for w in weights],
        out_specs=[row(d), row(att), col(att), grouped, grouped, col(att), row(pw), col(nh)],
        out_shape=[jax.ShapeDtypeStruct((n, d), F32),
                   jax.ShapeDtypeStruct((n, att), BF16),
                   jax.ShapeDtypeStruct((att, n), BF16),
                   jax.ShapeDtypeStruct((groups, att, rows), F32),
                   jax.ShapeDtypeStruct((groups, att, rows), F32),
                   jax.ShapeDtypeStruct((att, n), BF16),
                   jax.ShapeDtypeStruct((n, pw), F32),
                   jax.ShapeDtypeStruct((nh, n), F32)],
        compiler_params=_cparams("parallel"),
        name="ffn_in_proj",
    )(x, *weights)


def _attn_prompt_kernel(lf_ref, qt_ref, k_ref, vt_ref, o_ref, nb_scr, *, tile, head_dim):
    seq = k_ref.shape[0]
    c = _lane_cumsum(lf_ref[...]) * LOG2E
    for h in range(2):
        nb_scr[h] = -(jnp.broadcast_to(c[h:h + 1, :], (LANES, seq)).T)
    ones = jnp.ones((ONES_ROWS, tile), BF16)

    sub = lax.broadcasted_iota(jnp.int32, (2 * head_dim, tile), 0)
    key_i = lax.broadcasted_iota(jnp.int32, (tile, tile), 0)
    qry_i = lax.broadcasted_iota(jnp.int32, (tile, tile), 1)
    reps = tile // LANES

    for qi in range(seq // tile):
        q0 = qi * tile
        qt = qt_ref[:, pl.ds(q0, tile)]
        zero = jnp.zeros_like(qt)
        qt_heads = (jnp.where(sub < head_dim, qt, zero), jnp.where(sub < head_dim, zero, qt))

        def scores(j):
            k0 = j * tile
            kb = k_ref[pl.ds(k0, tile), :]
            return tuple(
                _dot(kb, qt_heads[h])
                + jnp.concatenate([nb_scr[h, pl.ds(k0, tile), :]] * reps, axis=1)
                for h in range(2))

        def absorb(j, s_pair, state):
            k0 = j * tile
            out = []
            for h in range(2):
                m, acc = state[h]
                s = s_pair[h]
                m_new = jnp.maximum(m, jnp.max(s, axis=0, keepdims=True))
                p = jnp.exp2(s - m_new)
                corr = jnp.exp2(m - m_new)
                vt = vt_ref[h * head_dim:(h + 1) * head_dim, pl.ds(k0, tile)]
                vt1 = jnp.concatenate([vt, ones], axis=0)
                acc = acc * corr + _dot(vt1, p.astype(BF16))
                out.append((m_new, acc))
            return tuple(out)

        state = tuple((jnp.full((1, tile), NEG_BIG, F32),
                       jnp.zeros((head_dim + ONES_ROWS, tile), F32)) for _ in range(2))
        for j in range(qi):
            state = absorb(j, scores(j), state)
        diag = tuple(jnp.where(key_i <= qry_i, s, NEG_BIG) for s in scores(qi))
        (_, a0), (_, a1) = absorb(qi, diag, state)
        o_t = jnp.concatenate(
            [a[:head_dim] * (1.0 / a[head_dim:head_dim + 1]) for a in (a0, a1)], axis=0)
        o_ref[pl.ds(q0, tile), :] = o_t.T.astype(o_ref.dtype)


def _attn_prompt(qt, kb, vtb, lf4, *, batch, seq, head_dim):
    n, att = kb.shape
    w = 2 * head_dim
    kern = functools.partial(_attn_prompt_kernel, tile=ATT_TILE, head_dim=head_dim)
    return pl.pallas_call(
        kern,
        grid=(batch, att // w),
        in_specs=[pl.BlockSpec((None, 2, seq), lambda b, h: (h, 0, b)),
                  pl.BlockSpec((w, seq), lambda b, h: (h, b)),
                  pl.BlockSpec((seq, w), lambda b, h: (b, h)),
                  pl.BlockSpec((w, seq), lambda b, h: (h, b))],
        out_specs=pl.BlockSpec((seq, w), lambda b, h: (b, h)),
        out_shape=jax.ShapeDtypeStruct((n, att), BF16),
        scratch_shapes=[pltpu.VMEM((2, seq, LANES), F32)],
        compiler_params=_cparams("parallel", "parallel"),
        name="attn_prompt",
    )(lf4, qt, kb, vtb)


def _attn_sample_kernel(pt_ref, q_ref, lfn_ref, kn_ref, vn_ref, *rest, pages, heads, head_dim):
    del pt_ref
    k_refs = rest[:pages]
    v_refs = rest[pages:2 * pages]
    lf_refs = rest[2 * pages:3 * pages]
    o_ref = rest[3 * pages]
    carry_scr, m_scr, l_scr, acc_scr = rest[3 * pages + 1:]
    g = pl.program_id(1)
    nq, att = q_ref.shape
    page = kn_ref.shape[-1]
    rows = heads * nq

    q_rep = jnp.concatenate([q_ref[...].astype(F32)] * heads, axis=0)
    r_i = lax.broadcasted_iota(jnp.int32, (rows, att), 0)
    c_i = lax.broadcasted_iota(jnp.int32, (rows, att), 1)
    q_bd = jnp.where(r_i // nq == c_i // head_dim, q_rep, 0.0).astype(BF16)

    def expand(x):
        return jnp.concatenate(
            [jnp.broadcast_to(x[h:h + 1, :], (nq, x.shape[1])) for h in range(heads)], axis=0)

    def process(krefs, vrefs, bias_fn, mask):
        s = jnp.concatenate([_dot(q_bd, kr[...].astype(BF16)) for kr in krefs], axis=1)
        s = s + bias_fn()
        if mask is not None:
            s = jnp.where(mask, s, NEG_BIG)
        m_old = m_scr[...]
        m_new = jnp.maximum(m_old, jnp.max(s, axis=-1, keepdims=True))
        p = jnp.exp(s - m_new)
        corr = jnp.exp(m_old - m_new)
        l_scr[...] = l_scr[...] * corr + jnp.sum(p, axis=-1, keepdims=True)
        m_scr[...] = m_new
        pb = p.astype(BF16)
        pv = None
        for j, vr in enumerate(vrefs):
            t = _dot_nt(pb[:, j * page:(j + 1) * page], vr[...].astype(BF16))
            pv = t if pv is None else pv + t
        acc_scr[...] = acc_scr[...] * corr + pv

    @pl.when(g == 0)
    def _():
        carry_scr[...] = jnp.zeros(carry_scr.shape, F32)
        m_scr[...] = jnp.full(m_scr.shape, NEG_BIG, F32)
        l_scr[...] = jnp.zeros(l_scr.shape, F32)
        acc_scr[...] = jnp.zeros(acc_scr.shape, F32)
        c_new = _lane_cumsum(lfn_ref[...])
        row = lax.broadcasted_iota(jnp.int32, (rows, page), 0)
        col = lax.broadcasted_iota(jnp.int32, (rows, page), 1)
        process([kn_ref], [vn_ref], lambda: -expand(c_new), col <= row % nq)

    def past_bias():
        carry = carry_scr[...]
        bias = [None] * pages
        for j in range(pages - 1, -1, -1):
            lf = lf_refs[j][...]
            incl = _lane_cumsum(lf, reverse=True)
            bias[j] = expand(incl - lf + carry)
            carry = carry + incl[:, 0:1]
        carry_scr[...] = carry
        return jnp.concatenate(bias, axis=1)

    process(k_refs, v_refs, past_bias, None)

    @pl.when(g == pl.num_programs(1) - 1)
    def _():
        out = acc_scr[...] * (1.0 / l_scr[...])
        o_ref[...] = jnp.concatenate(
            [out[h * nq:(h + 1) * nq, h * head_dim:(h + 1) * head_dim] for h in range(heads)],
            axis=1).astype(o_ref.dtype)


def _attn_sample(layer, page_table, qs, lfn_t, k_new, v_new, cache_kt, cache_vt, cache_lft):
    nb, nq, att = qs.shape
    heads, page = cache_lft.shape[2:]
    head_dim = att // heads
    n_pages = page_table.shape[1]
    pages = PAGES_PER_STEP
    groups = n_pages // pages

    def page_spec(j, tail):
        return pl.BlockSpec((None, None) + tail,
                            lambda b, g, pt: (layer, pt[b, (groups - 1 - g) * pages + j])
                            + (0,) * len(tail))

    def per_seq(shape):
        return pl.BlockSpec((None,) + shape, lambda b, g, pt: (b,) + (0,) * len(shape))

    kern = functools.partial(_attn_sample_kernel, pages=pages, heads=heads, head_dim=head_dim)
    grid_spec = pltpu.PrefetchScalarGridSpec(
        num_scalar_prefetch=1,
        grid=(nb, groups),
        in_specs=[per_seq((nq, att)), per_seq((heads, page)),
                  per_seq((att, page)), per_seq((att, page))]
                 + [page_spec(j, (att, page)) for j in range(pages)] * 2
                 + [page_spec(j, (heads, page)) for j in range(pages)],
        out_specs=per_seq((nq, att)),
        scratch_shapes=[pltpu.VMEM((heads, 1), F32),
                        pltpu.VMEM((heads * nq, 1), F32), pltpu.VMEM((heads * nq, 1), F32),
                        pltpu.VMEM((heads * nq, att), F32)],
    )
    return pl.pallas_call(
        kern,
        grid_spec=grid_spec,
        out_shape=jax.ShapeDtypeStruct((nb, nq, att), BF16),
        compiler_params=_cparams("parallel", "arbitrary"),
        name="attn_sample",
    )(page_table, qs, lfn_t, k_new, v_new,
      *([cache_kt] * pages), *([cache_vt] * pages), *([cache_lft] * pages))


def _window_means_minus_self(ext, u, pos, axis):
    halo = ext.shape[axis] - u.shape[axis]
    gw = u.shape[-1] // len(POOL_WINDOWS)
    outs = []
    for gi, w in enumerate(POOL_WINDOWS):
        a = ext[..., gi * gw:(gi + 1) * gw]
        d = 1
        while d < w:
            a = a + pltpu.roll(a, d, axis=axis)
            d *= 2
        a = lax.slice_in_dim(a, halo, ext.shape[axis], axis=axis)
        inv = 1.0 / jnp.minimum(pos + 1, w).astype(F32)
        outs.append(a * inv - u[..., gi * gw:(gi + 1) * gw])
    return outs


def _merge_tail(x, o_att, mixed, wg_ref, wgrp_ref, ps_ref, wba_ref, wbb_ref, wo_ref,
                g_ref, b_ref, *, alpha):
    xb = x.astype(BF16)
    d = x.shape[-1]
    gates = jax.nn.sigmoid(_dot(xb, wg_ref[...]))
    pooled = jnp.concatenate(
        [_dot(mx.astype(BF16), wgrp_ref[gi]) for gi, mx in enumerate(mixed)], axis=-1)
    pooled = pooled * ps_ref[...]
    merged = (gates[:, :d] * _dot(o_att, wba_ref[...])
              + gates[:, d:] * _dot(pooled.astype(BF16), wbb_ref[...]))
    y = alpha * x + _dot(merged.astype(BF16), wo_ref[...])
    return _layer_norm(y, g_ref[...], b_ref[...])


N_MERGE_WEIGHTS = 8


def _merge_ffn(x, o_att, mixed, weight_refs, *, alpha):
    x2 = _merge_tail(x, o_att, mixed, *weight_refs[:N_MERGE_WEIGHTS], alpha=alpha)
    return _ffn_ln_body(x2, *weight_refs[N_MERGE_WEIGHTS:], alpha=alpha)


def _merge_prompt_kernel(x_ref, oa_ref, u_ref, uh_ref, *rest, alpha, tiles_per_seq):
    o_ref = rest[-1]
    i = pl.program_id(0)
    t = i % tiles_per_seq
    u = u_ref[...]
    tm = u.shape[0]
    halo = jnp.where(t == 0, 0.0, uh_ref[...])
    ext = jnp.concatenate([halo, u], axis=0)
    pos = t * tm + lax.broadcasted_iota(jnp.int32, (tm, 1), 0)
    mixed = _window_means_minus_self(ext, u, pos, axis=0)
    o_ref[...] = _merge_ffn(x_ref[...], oa_ref[...], mixed, rest[:-1], alpha=alpha)


def _merge_sample_kernel(x_ref, oa_ref, u_ref, st_ref, *rest, alpha, start):
    o_ref = rest[-1]
    u = u_ref[...]
    nb, ns, ch = u.shape
    ext = jnp.concatenate([st_ref[...], u], axis=1)
    pos = start + lax.broadcasted_iota(jnp.int32, (1, ns, 1), 1)
    mixed = [m.reshape(nb * ns, -1) for m in _window_means_minus_self(ext, u, pos, axis=1)]
    o_ref[...] = _merge_ffn(x_ref[...], oa_ref[...], mixed, rest[:-1], alpha=alpha)


def _merge_weight_specs(ws):
    return [_const_spec(w.shape) for w in ws]


def _merge_prompt(x, o_att, u, weights, *, alpha, tm, seq):
    n, d = x.shape
    att = o_att.shape[1]
    pw = u.shape[1]
    hb = tm // POOL_HALO
    kern = functools.partial(_merge_prompt_kernel, alpha=alpha, tiles_per_seq=seq // tm)
    return pl.pallas_call(
        kern,
        grid=(n // tm,),
        in_specs=[pl.BlockSpec((tm, d), lambda i: (i, 0)),
                  pl.BlockSpec((tm, att), lambda i: (i, 0)),
                  pl.BlockSpec((tm, pw), lambda i: (i, 0)),
                  pl.BlockSpec((POOL_HALO, pw), lambda i: (jnp.maximum(i * hb - 1, 0), 0))]
                 + _merge_weight_specs(weights),
        out_specs=pl.BlockSpec((tm, d), lambda i: (i, 0)),
        out_shape=jax.ShapeDtypeStruct((n, d), F32),
        compiler_params=_cparams("parallel"),
        name="merge_prompt",
    )(x, o_att, u, u, *weights)


def _merge_sample(x, o_att, u3, state, weights, *, alpha, start):
    n, d = x.shape
    kern = functools.partial(_merge_sample_kernel, alpha=alpha, start=start)
    return pl.pallas_call(
        kern,
        grid=(1,),
        in_specs=[_const_spec(x.shape), _const_spec(o_att.shape), _const_spec(u3.shape),
                  _const_spec(state.shape)] + _merge_weight_specs(weights),
        out_specs=_const_spec((n, d)),
        out_shape=jax.ShapeDtypeStruct((n, d), F32),
        compiler_params=_cparams("arbitrary"),
        name="merge_sample",
    )(x, o_att, u3, state, *weights)


def kernel(x_prompt, x_sample, cache_k, cache_v, cache_logf, state_pool, page_table,
           ln_g, ln_b, w_ffn_up, w_ffn_down, w_in, b_f, w_pool_grp, pool_scale,
           w_branch_a, w_branch_b, w_out):
    batch, seq, d = x_prompt.shape
    dec_b, dec_s, _ = x_sample.shape
    depth, _, page_size, heads, head_dim = cache_k.shape
    att = heads * head_dim
    pool_w = pool_scale.shape[1]
    pool_buf = state_pool.shape[2]
    n_pages = page_table.shape[1]
    past = n_pages * page_size
    alpha = (2.0 * depth) ** 0.25
    scale = head_dim ** -0.5
    np_rows = batch * seq
    ns_rows = dec_b * dec_s
    assert pool_buf == max(POOL_WINDOWS) - 1 and pool_w % (len(POOL_WINDOWS) * LANES) == 0
    assert seq % ROW_TILE == 0 and seq % ATT_TILE == 0 and n_pages % PAGES_PER_STEP == 0
    assert dec_s == SUBLANES and heads % 2 == 0 and 2 * head_dim == LANES and page_size == LANES

    pool_pages = cache_k.shape[1]
    cache_kt = cache_k.transpose(0, 1, 3, 4, 2).reshape(depth, pool_pages, att, page_size)
    cache_vt = cache_v.transpose(0, 1, 3, 4, 2).reshape(depth, pool_pages, att, page_size)
    cache_lft = cache_logf.transpose(0, 1, 3, 2)

    xp = x_prompt.reshape(np_rows, d)
    xs = x_sample.reshape(ns_rows, d)
    outs = {k: [] for k in ("kp", "vp", "fp", "bp", "ks", "vs", "fs", "bs")}

    for l in range(depth):
        wu = w_ffn_up[l].astype(BF16)
        wd = w_ffn_down[l].astype(BF16)
        wi = w_in[l]
        w_q = wi[:, :att].astype(BF16)
        w_k = wi[:, att:2 * att].astype(BF16)
        w_t = wi[:, :3 * att].T.astype(BF16)
        w_ft = wi[:, 3 * att:3 * att + heads].T.astype(BF16)
        c0 = 3 * att + heads
        w_u = wi[:, c0:c0 + pool_w].astype(BF16)
        w_g = wi[:, c0 + pool_w:].astype(BF16)
        vec = lambda a: a.reshape(1, d)
        ffn0_w = (wu[0], wd[0], vec(ln_g[l, 0]), vec(ln_b[l, 0]))
        merge_w = (w_g, w_pool_grp[l].astype(BF16), pool_scale[l].reshape(1, pool_w),
                   w_branch_a[l].astype(BF16), w_branch_b[l].astype(BF16),
                   w_out[l].astype(BF16), vec(ln_g[l, 1]), vec(ln_b[l, 1]),
                   wu[1], wd[1], vec(ln_g[l, 2]), vec(ln_b[l, 2]))
        assert len(merge_w) == N_MERGE_WEIGHTS + 4

        x1, kb, qt, kt, vt, vtb, u, lft = _ffn_in_proj(
            xp, ffn0_w, w_k, w_t, w_u, w_ft, b_f[l], tm=ROW_TILE, groups=batch,
            alpha=alpha, scale=scale * LOG2E, row_scale=1.0)
        o_att = _attn_prompt(qt, kb, vtb, lft.reshape(heads // 2, 2, np_rows),
                             batch=batch, seq=seq, head_dim=head_dim)
        xp = _merge_prompt(x1, o_att, u, merge_w, alpha=alpha, tm=ROW_TILE, seq=seq)
        to_out = lambda a: a.reshape(batch, heads, head_dim, seq).transpose(0, 3, 1, 2)
        outs["kp"].append(to_out(kt))
        outs["vp"].append(to_out(vt))
        outs["fp"].append(lft.reshape(heads, batch, seq).transpose(1, 2, 0))
        outs["bp"].append(u.reshape(batch, seq, pool_w)[:, seq - pool_buf:])

        y1, qb, _, kt, vt, _, u, lft = _ffn_in_proj(
            xs, ffn0_w, w_q, w_t, w_u, w_ft, b_f[l], tm=ns_rows, groups=1,
            alpha=alpha, scale=scale, row_scale=scale)
        lf_new = lft.reshape(heads, dec_b, dec_s).transpose(1, 0, 2)
        lfn_t = jnp.pad(lf_new, ((0, 0), (0, 0), (0, page_size - dec_s)))
        kt4 = kt.reshape(heads, head_dim, dec_b, dec_s)
        vt4 = vt.reshape(heads, head_dim, dec_b, dec_s)
        as_page = lambda a: jnp.pad(a.reshape(att, dec_b, dec_s).transpose(1, 0, 2),
                                    ((0, 0), (0, 0), (0, page_size - dec_s)))
        o_att = _attn_sample(l, page_table, qb.reshape(dec_b, dec_s, att), lfn_t,
                             as_page(kt), as_page(vt), cache_kt, cache_vt, cache_lft)
        state = jnp.pad(state_pool[l], ((0, 0), (POOL_HALO - pool_buf, 0), (0, 0)))
        u3 = u.reshape(dec_b, dec_s, pool_w)
        xs = _merge_sample(y1, o_att.reshape(ns_rows, att), u3, state, merge_w,
                           alpha=alpha, start=past)
        outs["ks"].append(kt4.transpose(2, 3, 0, 1))
        outs["vs"].append(vt4.transpose(2, 3, 0, 1))
        outs["fs"].append(lf_new.transpose(0, 2, 1))
        outs["bs"].append(jnp.concatenate([state_pool[l], u3], axis=1)[:, dec_s:])

    st = {k: jnp.stack(v) for k, v in outs.items()}
    return (xp.reshape(batch, seq, d), xs.reshape(dec_b, dec_s, d),
            st["kp"], st["vp"], st["fp"], st["bp"], st["ks"], st["vs"], st["fs"], st["bs"])
```

```python
import functools

import jax
import jax.numpy as jnp
from jax import lax
from jax.experimental import pallas as pl
from jax.experimental.pallas import tpu as pltpu

F32 = jnp.float32
BF16 = jnp.bfloat16

LANES = 128
SUBLANES = 8
VMEM_LIMIT_BYTES = 56 * 1024 * 1024

POOL_WINDOWS = (2, 4, 8, 16)
POOL_HALO = 16
LN_EPS = 1e-5
NEG_BIG = -1e30

ROW_TILE = 512
ATT_TILE = 512
ONES_ROWS = 16
LOG2E = 1.4426950408889634
PAGES_PER_STEP = 16


def _cparams(*sem):
    return pltpu.CompilerParams(dimension_semantics=sem, vmem_limit_bytes=VMEM_LIMIT_BYTES)


def _const_spec(shape):
    nd = len(shape)
    return pl.BlockSpec(shape, lambda *_: (0,) * nd)


def _layer_norm(y, g, b):
    mu = jnp.mean(y, axis=-1, keepdims=True)
    d = y - mu
    var = jnp.mean(d * d, axis=-1, keepdims=True)
    return d * lax.rsqrt(var + LN_EPS) * g + b


def _dot(a, b):
    return jnp.dot(a, b, preferred_element_type=F32)


def _dot_nt(a, b):
    return lax.dot_general(a, b, (((1,), (1,)), ((), ())), preferred_element_type=F32)


def _lane_cumsum(x, reverse=False):
    rows, length = x.shape
    lane = lax.broadcasted_iota(jnp.int32, (rows, LANES), 1)
    n = length // LANES
    order = range(n - 1, -1, -1) if reverse else range(n)
    out = [None] * n
    carry = jnp.zeros((rows, 1), F32)
    for c in order:
        v = x[:, c * LANES:(c + 1) * LANES]
        d = 1
        while d < LANES:
            if reverse:
                v = v + jnp.where(lane < LANES - d, pltpu.roll(v, LANES - d, axis=1), 0.0)
            else:
                v = v + jnp.where(lane >= d, pltpu.roll(v, d, axis=1), 0.0)
            d *= 2
        v = v + carry
        carry = v[:, 0:1] if reverse else v[:, LANES - 1:LANES]
        out[c] = v
    return jnp.concatenate(out, axis=1)


FFN_CHUNK = 256


def _ffn_ln_body(x, wu_ref, wd_ref, g_ref, b_ref, *, alpha):
    d_ff = wd_ref.shape[0]
    xb = x.astype(BF16)
    acc = jnp.zeros(x.shape, F32)
    for lo in range(0, d_ff, FFN_CHUNK):
        gate = _dot(xb, wu_ref[:, lo:lo + FFN_CHUNK])
        up = _dot(xb, wu_ref[:, d_ff + lo:d_ff + lo + FFN_CHUNK])
        act = (gate * jax.nn.sigmoid(gate)) * up
        acc = acc + _dot(act.astype(BF16), wd_ref[lo:lo + FFN_CHUNK, :])
    return _layer_norm(alpha * x + 0.5 * acc, g_ref[...], b_ref[...])


def _ffn_in_proj_kernel(x_ref, wu_ref, wd_ref, g_ref, b_ref, wrow_ref, wt_ref, wpu_ref, wft_ref,
                        bf_ref, kt_all_ref, vt_all_ref,
                        x1_ref, row_ref, qt_ref, kt_ref, vt_ref, vtb_ref, u_ref, lft_ref,
                        *, att, alpha, scale, row_scale):
    del kt_all_ref, vt_all_ref
    x1 = _ffn_ln_body(x_ref[...], wu_ref, wd_ref, g_ref, b_ref, alpha=alpha)
    x1_ref[...] = x1
    xb = x1.astype(BF16)
    row_ref[...] = (_dot(xb, wrow_ref[...]) * row_scale).astype(BF16)
    qkv_t = _dot_nt(wt_ref[...], xb)
    qt_ref[...] = (qkv_t[:att] * scale).astype(BF16)
    kt_ref[...] = qkv_t[att:2 * att]
    v_t = qkv_t[2 * att:]
    vt_ref[...] = v_t
    vtb_ref[...] = v_t.astype(BF16)
    u_ref[...] = _dot(xb, wpu_ref[...])
    f = _dot_nt(wft_ref[...], xb) + bf_ref[...]
    lft_ref[...] = jnp.minimum(f, 0.0) - jnp.log1p(jnp.exp(-jnp.abs(f)))


def _ffn_in_proj(x, ffn_w, w_row, w_t, w_u, w_ft, b_f, kt_all, vt_all, *, layer, tm, alpha,
                 scale, row_scale):
    n, d = x.shape
    att = w_t.shape[0] // 3
    pw = w_u.shape[1]
    nh = w_ft.shape[0]
    _, groups, _, rows = kt_all.shape
    tiles = rows // tm
    row = lambda w: pl.BlockSpec((tm, w), lambda i: (i, 0))
    col = lambda h: pl.BlockSpec((h, tm), lambda i: (0, i))
    layer_block = pl.BlockSpec((None, None, att, tm),
                               lambda i: (layer, i // tiles, 0, i % tiles))
    untouched = pl.BlockSpec(memory_space=pl.ANY)
    weights = (*ffn_w, w_row, w_t, w_u, w_ft, b_f.reshape(nh, 1))
    kern = functools.partial(_ffn_in_proj_kernel, att=att, alpha=alpha, scale=scale,
                             row_scale=row_scale)
    n_in = 1 + len(weights)
    return pl.pallas_call(
        kern,
        grid=(n // tm,),
        in_specs=[row(d)] + [_const_spec(w.shape) for w in weights] + [untouched, untouched],
        out_specs=[row(d), row(att), col(att), layer_block, layer_block, col(att), row(pw),
                   col(nh)],
        out_shape=[jax.ShapeDtypeStruct((n, d), F32),
                   jax.ShapeDtypeStruct((n, att), BF16),
                   jax.ShapeDtypeStruct((att, n), BF16),
                   jax.ShapeDtypeStruct(kt_all.shape, F32),
                   jax.ShapeDtypeStruct(vt_all.shape, F32),
                   jax.ShapeDtypeStruct((att, n), BF16),
                   jax.ShapeDtypeStruct((n, pw), F32),
                   jax.ShapeDtypeStruct((nh, n), F32)],
        input_output_aliases={n_in: 3, n_in + 1: 4},
        compiler_params=_cparams("parallel"),
        name="ffn_in_proj",
    )(x, *weights, kt_all, vt_all)


def _attn_prompt_kernel(lf_ref, qt_ref, k_ref, vt_ref, o_ref, nb_scr, *, tile, head_dim):
    seq = k_ref.shape[0]
    c = _lane_cumsum(lf_ref[...]) * LOG2E
    for h in range(2):
        nb_scr[h] = -(jnp.broadcast_to(c[h:h + 1, :], (LANES, seq)).T)
    sub = lax.broadcasted_iota(jnp.int32, (2 * head_dim, tile), 0)

    def causal(s):
        key_i = lax.broadcasted_iota(jnp.int32, s.shape, 0)
        qry_i = lax.broadcasted_iota(jnp.int32, s.shape, 1)
        return jnp.where(key_i <= qry_i, s, NEG_BIG)

    for qi in range(seq // tile):
        q0 = qi * tile
        qt = qt_ref[:, pl.ds(q0, tile)]
        zero = jnp.zeros_like(qt)
        qt_heads = (jnp.where(sub < head_dim, qt, zero), jnp.where(sub < head_dim, zero, qt))

        def scores(k0, nk, qa=0):
            kb = k_ref[pl.ds(k0, nk), :]
            reps = (tile - qa) // LANES
            return tuple(
                _dot(kb, qt_heads[h][:, qa:])
                + jnp.concatenate([nb_scr[h, pl.ds(k0, nk), :]] * reps, axis=1)
                for h in range(2))

        def absorb(k0, nk, s_pair, state):
            ones = jnp.ones((ONES_ROWS, nk), BF16)
            out = []
            for h in range(2):
                m, acc = state[h]
                s = s_pair[h]
                m_new = jnp.maximum(m, jnp.max(s, axis=0, keepdims=True))
                p = jnp.exp2(s - m_new)
                corr = jnp.exp2(m - m_new)
                vt = vt_ref[h * head_dim:(h + 1) * head_dim, pl.ds(k0, nk)]
                vt1 = jnp.concatenate([vt, ones], axis=0)
                acc = acc * corr + _dot(vt1, p.astype(BF16))
                out.append((m_new, acc))
            return tuple(out)

        state = tuple((jnp.full((1, tile), NEG_BIG, F32),
                       jnp.zeros((head_dim + ONES_ROWS, tile), F32)) for _ in range(2))
        for j in range(qi):
            state = absorb(j * tile, tile, scores(j * tile, tile), state)
        (_, a0), (_, a1) = absorb(q0, tile, tuple(causal(s) for s in scores(q0, tile)), state)
        o_t = jnp.concatenate(
            [a[:head_dim] * (1.0 / a[head_dim:head_dim + 1]) for a in (a0, a1)], axis=0)
        o_ref[pl.ds(q0, tile), :] = o_t.T.astype(o_ref.dtype)


def _attn_prompt(qt, kb, vtb, lf4, *, batch, seq, head_dim):
    n, att = kb.shape
    w = 2 * head_dim
    kern = functools.partial(_attn_prompt_kernel, tile=ATT_TILE, head_dim=head_dim)
    return pl.pallas_call(
        kern,
        grid=(batch, att // w),
        in_specs=[pl.BlockSpec((None, 2, seq), lambda b, h: (h, 0, b)),
                  pl.BlockSpec((w, seq), lambda b, h: (h, b)),
                  pl.BlockSpec((seq, w), lambda b, h: (b, h)),
                  pl.BlockSpec((w, seq), lambda b, h: (h, b))],
        out_specs=pl.BlockSpec((seq, w), lambda b, h: (b, h)),
        out_shape=jax.ShapeDtypeStruct((n, att), BF16),
        scratch_shapes=[pltpu.VMEM((2, seq, LANES), F32)],
        compiler_params=_cparams("parallel", "parallel"),
        name="attn_prompt",
    )(lf4, qt, kb, vtb)


def _attn_sample_kernel(pt_ref, q_ref, lfn_ref, kn_ref, vn_ref, *rest, pages, heads, head_dim):
    del pt_ref
    k_refs = rest[:pages]
    v_refs = rest[pages:2 * pages]
    lf_refs = rest[2 * pages:3 * pages]
    o_ref = rest[3 * pages]
    carry_scr, m_scr, l_scr, acc_scr = rest[3 * pages + 1:]
    b = pl.program_id(0)
    g = pl.program_id(1)
    nq, att = q_ref.shape
    page = lf_refs[0].shape[-1]
    rows = heads * nq

    q_rep = jnp.concatenate([q_ref[...].astype(F32)] * heads, axis=0)
    r_i = lax.broadcasted_iota(jnp.int32, (rows, att), 0)
    c_i = lax.broadcasted_iota(jnp.int32, (rows, att), 1)
    q_bd = jnp.where(r_i // nq == c_i // head_dim, q_rep, 0.0).astype(BF16)

    def expand(x):
        return jnp.concatenate(
            [jnp.broadcast_to(x[h:h + 1, :], (nq, x.shape[1])) for h in range(heads)], axis=0)

    def process(k_pages, v_pages, bias_fn, mask):
        s = jnp.concatenate([_dot(q_bd, kp.astype(BF16)) for kp in k_pages], axis=1)
        s = s + bias_fn()
        if mask is not None:
            s = jnp.where(mask, s, NEG_BIG)
        m_old = m_scr[...]
        m_new = jnp.maximum(m_old, jnp.max(s, axis=-1, keepdims=True))
        p = jnp.exp(s - m_new)
        corr = jnp.exp(m_old - m_new)
        l_scr[...] = l_scr[...] * corr + jnp.sum(p, axis=-1, keepdims=True)
        m_scr[...] = m_new
        pb = p.astype(BF16)
        pv = None
        for j, vp in enumerate(v_pages):
            t = _dot_nt(pb[:, j * page:(j + 1) * page], vp.astype(BF16))
            pv = t if pv is None else pv + t
        acc_scr[...] = acc_scr[...] * corr + pv

    @pl.when(g == 0)
    def _():
        carry_scr[...] = jnp.zeros(carry_scr.shape, F32)
        m_scr[...] = jnp.full(m_scr.shape, NEG_BIG, F32)
        l_scr[...] = jnp.zeros(l_scr.shape, F32)
        acc_scr[...] = jnp.zeros(acc_scr.shape, F32)
        per_chunk = page // nq
        chunk = pl.ds(pl.multiple_of((b // per_chunk) * page, page), page)
        off = (b % per_chunk) * nq
        lane = lax.broadcasted_iota(jnp.int32, (heads, page), 1)
        mine = (lane >= off) & (lane < off + nq)
        c_new = _lane_cumsum(jnp.where(mine, lfn_ref[:, chunk], 0.0))
        row = lax.broadcasted_iota(jnp.int32, (rows, page), 0)
        col = lax.broadcasted_iota(jnp.int32, (rows, page), 1) - off
        process([kn_ref[:, chunk]], [vn_ref[:, chunk]], lambda: -expand(c_new),
                (col >= 0) & (col <= row % nq))

    def past_bias():
        carry = carry_scr[...]
        bias = [None] * pages
        for j in range(pages - 1, -1, -1):
            lf = lf_refs[j][...]
            incl = _lane_cumsum(lf, reverse=True)
            bias[j] = expand(incl - lf + carry)
            carry = carry + incl[:, 0:1]
        carry_scr[...] = carry
        return jnp.concatenate(bias, axis=1)

    process([r[...] for r in k_refs], [r[...] for r in v_refs], past_bias, None)

    @pl.when(g == pl.num_programs(1) - 1)
    def _():
        out = acc_scr[...] * (1.0 / l_scr[...])
        o_ref[...] = jnp.concatenate(
            [out[h * nq:(h + 1) * nq, h * head_dim:(h + 1) * head_dim] for h in range(heads)],
            axis=1).astype(o_ref.dtype)


def _attn_sample(layer, page_table, qs, lfn_t, k_new, v_new, cache_kt, cache_vt, cache_lft):
    nb, nq, att = qs.shape
    assert LANES % nq == 0 and (nb * nq) % LANES == 0
    new_spec = pl.BlockSpec((None, None, att, nb * nq), lambda b, g, pt: (layer, 0, 0, 0))
    heads, page = cache_lft.shape[2:]
    head_dim = att // heads
    n_pages = page_table.shape[1]
    pages = PAGES_PER_STEP
    groups = n_pages // pages

    def page_spec(j, tail):
        return pl.BlockSpec((None, None) + tail,
                            lambda b, g, pt: (layer, pt[b, (groups - 1 - g) * pages + j])
                            + (0,) * len(tail))

    def per_seq(shape):
        return pl.BlockSpec((None,) + shape, lambda b, g, pt: (b,) + (0,) * len(shape))

    kern = functools.partial(_attn_sample_kernel, pages=pages, heads=heads, head_dim=head_dim)
    grid_spec = pltpu.PrefetchScalarGridSpec(
        num_scalar_prefetch=1,
        grid=(nb, groups),
        in_specs=[per_seq((nq, att)), _const_spec(lfn_t.shape), new_spec, new_spec]
                 + [page_spec(j, (att, page)) for j in range(pages)] * 2
                 + [page_spec(j, (heads, page)) for j in range(pages)],
        out_specs=per_seq((nq, att)),
        scratch_shapes=[pltpu.VMEM((heads, 1), F32),
                        pltpu.VMEM((heads * nq, 1), F32), pltpu.VMEM((heads * nq, 1), F32),
                        pltpu.VMEM((heads * nq, att), F32)],
    )
    return pl.pallas_call(
        kern,
        grid_spec=grid_spec,
        out_shape=jax.ShapeDtypeStruct((nb, nq, att), BF16),
        compiler_params=_cparams("parallel", "arbitrary"),
        name="attn_sample",
    )(page_table, qs, lfn_t, k_new, v_new,
      *([cache_kt] * pages), *([cache_vt] * pages), *([cache_lft] * pages))


def _window_means_minus_self(ext, u, pos, axis):
    halo = ext.shape[axis] - u.shape[axis]
    gw = u.shape[-1] // len(POOL_WINDOWS)
    outs = []
    for gi, w in enumerate(POOL_WINDOWS):
        a = ext[..., gi * gw:(gi + 1) * gw]
        d = 1
        while d < w:
            a = a + pltpu.roll(a, d, axis=axis)
            d *= 2
        a = lax.slice_in_dim(a, halo, ext.shape[axis], axis=axis)
        inv = 1.0 / jnp.minimum(pos + 1, w).astype(F32)
        outs.append(a * inv - u[..., gi * gw:(gi + 1) * gw])
    return outs


def _merge_tail(x, o_att, mixed, wg_ref, wgrp_ref, ps_ref, wba_ref, wbb_ref, wo_ref,
                g_ref, b_ref, *, alpha):
    xb = x.astype(BF16)
    d = x.shape[-1]
    gates = jax.nn.sigmoid(_dot(xb, wg_ref[...]))
    pooled = jnp.concatenate(
        [_dot(mx.astype(BF16), wgrp_ref[gi]) for gi, mx in enumerate(mixed)], axis=-1)
    pooled = pooled * ps_ref[...]
    merged = (gates[:, :d] * _dot(o_att, wba_ref[...])
              + gates[:, d:] * _dot(pooled.astype(BF16), wbb_ref[...]))
    y = alpha * x + _dot(merged.astype(BF16), wo_ref[...])
    return _layer_norm(y, g_ref[...], b_ref[...])


N_MERGE_WEIGHTS = 8


def _merge_ffn(x, o_att, mixed, weight_refs, *, alpha):
    x2 = _merge_tail(x, o_att, mixed, *weight_refs[:N_MERGE_WEIGHTS], alpha=alpha)
    return _ffn_ln_body(x2, *weight_refs[N_MERGE_WEIGHTS:], alpha=alpha)


def _merge_prompt_kernel(x_ref, oa_ref, u_ref, uh_ref, *rest, alpha, tiles_per_seq):
    o_ref = rest[-1]
    i = pl.program_id(0)
    t = i % tiles_per_seq
    u = u_ref[...]
    tm = u.shape[0]
    halo = jnp.where(t == 0, 0.0, uh_ref[...])
    ext = jnp.concatenate([halo, u], axis=0)
    pos = t * tm + lax.broadcasted_iota(jnp.int32, (tm, 1), 0)
    mixed = _window_means_minus_self(ext, u, pos, axis=0)
    o_ref[...] = _merge_ffn(x_ref[...], oa_ref[...], mixed, rest[:-1], alpha=alpha)


def _merge_sample_kernel(x_ref, oa_ref, u_ref, st_ref, *rest, alpha, start):
    o_ref = rest[-1]
    u = u_ref[...]
    nb, ns, ch = u.shape
    ext = jnp.concatenate([st_ref[...], u], axis=1)
    pos = start + lax.broadcasted_iota(jnp.int32, (1, ns, 1), 1)
    mixed = [m.reshape(nb * ns, -1) for m in _window_means_minus_self(ext, u, pos, axis=1)]
    o_ref[...] = _merge_ffn(x_ref[...], oa_ref[...], mixed, rest[:-1], alpha=alpha)


def _merge_weight_specs(ws):
    return [_const_spec(w.shape) for w in ws]


def _merge_prompt(x, o_att, u, weights, *, alpha, tm, seq):
    n, d = x.shape
    att = o_att.shape[1]
    pw = u.shape[1]
    hb = tm // POOL_HALO
    kern = functools.partial(_merge_prompt_kernel, alpha=alpha, tiles_per_seq=seq // tm)
    return pl.pallas_call(
        kern,
        grid=(n // tm,),
        in_specs=[pl.BlockSpec((tm, d), lambda i: (i, 0)),
                  pl.BlockSpec((tm, att), lambda i: (i, 0)),
                  pl.BlockSpec((tm, pw), lambda i: (i, 0)),
                  pl.BlockSpec((POOL_HALO, pw), lambda i: (jnp.maximum(i * hb - 1, 0), 0))]
                 + _merge_weight_specs(weights),
        out_specs=pl.BlockSpec((tm, d), lambda i: (i, 0)),
        out_shape=jax.ShapeDtypeStruct((n, d), F32),
        compiler_params=_cparams("parallel"),
        name="merge_prompt",
    )(x, o_att, u, u, *weights)


def _merge_sample(x, o_att, u3, state, weights, *, alpha, start):
    n, d = x.shape
    kern = functools.partial(_merge_sample_kernel, alpha=alpha, start=start)
    return pl.pallas_call(
        kern,
        grid=(1,),
        in_specs=[_const_spec(x.shape), _const_spec(o_att.shape), _const_spec(u3.shape),
                  _const_spec(state.shape)] + _merge_weight_specs(weights),
        out_specs=_const_spec((n, d)),
        out_shape=jax.ShapeDtypeStruct((n, d), F32),
        compiler_params=_cparams("arbitrary"),
        name="merge_sample",
    )(x, o_att, u3, state, *weights)


def kernel(x_prompt, x_sample, cache_k, cache_v, cache_logf, state_pool, page_table,
           ln_g, ln_b, w_ffn_up, w_ffn_down, w_in, b_f, w_pool_grp, pool_scale,
           w_branch_a, w_branch_b, w_out):
    batch, seq, d = x_prompt.shape
    dec_b, dec_s, _ = x_sample.shape
    depth, _, page_size, heads, head_dim = cache_k.shape
    att = heads * head_dim
    pool_w = pool_scale.shape[1]
    pool_buf = state_pool.shape[2]
    n_pages = page_table.shape[1]
    past = n_pages * page_size
    alpha = (2.0 * depth) ** 0.25
    scale = head_dim ** -0.5
    np_rows = batch * seq
    ns_rows = dec_b * dec_s
    assert pool_buf == max(POOL_WINDOWS) - 1 and pool_w % (len(POOL_WINDOWS) * LANES) == 0
    assert seq % ROW_TILE == 0 and seq % ATT_TILE == 0 and n_pages % PAGES_PER_STEP == 0
    assert dec_s == SUBLANES and heads % 2 == 0 and 2 * head_dim == LANES and page_size == LANES

    pool_pages = cache_k.shape[1]
    cache_kt = cache_k.transpose(0, 1, 3, 4, 2).reshape(depth, pool_pages, att, page_size)
    cache_vt = cache_v.transpose(0, 1, 3, 4, 2).reshape(depth, pool_pages, att, page_size)
    cache_lft = cache_logf.transpose(0, 1, 3, 2)

    xp = x_prompt.reshape(np_rows, d)
    xs = x_sample.reshape(ns_rows, d)
    outs = {k: [] for k in ("fp", "bp", "fs", "bs")}
    ktp = jnp.zeros((depth, batch, att, seq), F32)
    vtp = jnp.zeros((depth, batch, att, seq), F32)
    kts = jnp.zeros((depth, 1, att, ns_rows), F32)
    vts = jnp.zeros((depth, 1, att, ns_rows), F32)
    state_all = jnp.pad(state_pool, ((0, 0), (0, 0), (POOL_HALO - pool_buf, 0), (0, 0)))

    for l in range(depth):
        wu = w_ffn_up[l].astype(BF16)
        wd = w_ffn_down[l].astype(BF16)
        wi = w_in[l]
        w_q = wi[:, :att].astype(BF16)
        w_k = wi[:, att:2 * att].astype(BF16)
        w_t = wi[:, :3 * att].T.astype(BF16)
        w_ft = wi[:, 3 * att:3 * att + heads].T.astype(BF16)
        c0 = 3 * att + heads
        w_u = wi[:, c0:c0 + pool_w].astype(BF16)
        w_g = wi[:, c0 + pool_w:].astype(BF16)
        vec = lambda a: a.reshape(1, d)
        ffn0_w = (wu[0], wd[0], vec(ln_g[l, 0]), vec(ln_b[l, 0]))
        merge_w = (w_g, w_pool_grp[l].astype(BF16), pool_scale[l].reshape(1, pool_w),
                   w_branch_a[l].astype(BF16), w_branch_b[l].astype(BF16),
                   w_out[l].astype(BF16), vec(ln_g[l, 1]), vec(ln_b[l, 1]),
                   wu[1], wd[1], vec(ln_g[l, 2]), vec(ln_b[l, 2]))
        assert len(merge_w) == N_MERGE_WEIGHTS + 4

        x1, kb, qt, ktp, vtp, vtb, u, lft = _ffn_in_proj(
            xp, ffn0_w, w_k, w_t, w_u, w_ft, b_f[l], ktp, vtp, layer=l, tm=ROW_TILE,
            alpha=alpha, scale=scale * LOG2E, row_scale=1.0)
        o_att = _attn_prompt(qt, kb, vtb, lft.reshape(heads // 2, 2, np_rows),
                             batch=batch, seq=seq, head_dim=head_dim)
        xp = _merge_prompt(x1, o_att, u, merge_w, alpha=alpha, tm=ROW_TILE, seq=seq)
        outs["fp"].append(lft)
        outs["bp"].append(u.reshape(batch, seq, pool_w)[:, seq - pool_buf:])

        y1, qb, _, kts, vts, _, u, lft = _ffn_in_proj(
            xs, ffn0_w, w_q, w_t, w_u, w_ft, b_f[l], kts, vts, layer=l, tm=ns_rows,
            alpha=alpha, scale=scale, row_scale=scale)
        o_att = _attn_sample(l, page_table, qb.reshape(dec_b, dec_s, att), lft, kts, vts,
                             cache_kt, cache_vt, cache_lft)
        u3 = u.reshape(dec_b, dec_s, pool_w)
        xs = _merge_sample(y1, o_att.reshape(ns_rows, att), u3, state_all[l], merge_w,
                           alpha=alpha, start=past)
        outs["fs"].append(lft)
        outs["bs"].append(u3)

    kv_prompt = lambda a: a.reshape(depth, batch, heads, head_dim, seq).transpose(0, 1, 4, 2, 3)
    kv_sample = lambda a: a.reshape(depth, heads, head_dim, dec_b, dec_s).transpose(0, 3, 4, 1, 2)
    logf_p = jnp.stack(outs["fp"]).reshape(depth, heads, batch, seq).transpose(0, 2, 3, 1)
    logf_s = jnp.stack(outs["fs"]).reshape(depth, heads, dec_b, dec_s).transpose(0, 2, 3, 1)
    pool_s = jnp.concatenate([state_pool, jnp.stack(outs["bs"])], axis=2)[:, :, dec_s:]
    return (xp.reshape(batch, seq, d), xs.reshape(dec_b, dec_s, d),
            kv_prompt(ktp), kv_prompt(vtp), logf_p, jnp.stack(outs["bp"]),
            kv_sample(kts), kv_sample(vts), logf_s, pool_s)
```
